```python
import math
import jax, jax.numpy as jnp
from jax import lax
import numpy as np

D_MODEL = 1024
BATCH = 8
SEQ = 4096
DEPTH = 2

CHUNK = 64

CONV_WIDTH = D_MODEL // 2
CONV_GROUPS = 8
CONV_K = 3
SGU_WIDTH = D_MODEL // 2
SGU_HEADS = 8
SGU_HEAD_DIM = SGU_WIDTH // SGU_HEADS
SGU_BLOCK = 128
IN_COLS = 3 * CONV_WIDTH + 2 * SGU_WIDTH

SB_HEADS = 16
SB_HEAD_DIM = D_MODEL // SB_HEADS
SB_QBLOCK = 128

N_GROUPS = 4
EXPERTS_PER_GROUP = 4
TOP_K_IN_GROUP = 2
D_EXPERT = D_MODEL // 2

DN_ALPHA = (2 * DEPTH) ** 0.25
DN_BETA = (8 * DEPTH) ** -0.25
LN_EPS = 1e-5

N_EVEN = (DEPTH + 1) // 2
N_ODD = DEPTH // 2

kernel_name = "hybrid_conv_sgu_stickbreak_hmoe_deepnorm"


def layer_norm(x, g, b):
    xf = x.astype(jnp.float32)
    mu = jnp.mean(xf, axis=-1, keepdims=True)
    var = jnp.mean(jnp.square(xf - mu), axis=-1, keepdims=True)
    y = (xf - mu) * lax.rsqrt(var + LN_EPS) * g.astype(jnp.float32) + b.astype(jnp.float32)
    return y.astype(x.dtype)


def short_conv_mix(b_gate, c_gate, h, conv_w):
    u = c_gate * h
    y = lax.conv_general_dilated(
        u, conv_w[:, None, :].astype(u.dtype),
        window_strides=(1,), padding=[(CONV_K - 1, 0)],
        dimension_numbers=("NWC", "WIO", "NWC"),
        feature_group_count=CONV_WIDTH)
    return b_gate * y


def spatial_gating(z_u, z_v, ln_g, ln_b, w_s, b_s):
    bsz, seq, _ = z_v.shape
    v = layer_norm(z_v, ln_g, ln_b)
    vb = v.reshape(bsz, seq // SGU_BLOCK, SGU_BLOCK, SGU_HEADS, SGU_HEAD_DIM)
    pos = jnp.arange(SGU_BLOCK)
    chunk_causal = (pos[None, :] // CHUNK) <= (pos[:, None] // CHUNK)
    w = jnp.where(chunk_causal[None], w_s, jnp.zeros((), w_s.dtype))
    mixed = jnp.einsum("hts,bnshc->bnthc", w, vb) + jnp.transpose(b_s)[None, None, :, :, None]
    return z_u * mixed.reshape(bsz, seq, SGU_WIDTH)


def conv_sgu_mixer(x, w_in, conv_w, ln_g, ln_b, w_s, b_s, w_out):
    proj = x @ w_in
    b_gate, c_gate, h, z_sgu = jnp.split(
        proj, [CONV_WIDTH, 2 * CONV_WIDTH, 3 * CONV_WIDTH], axis=-1)
    z_u, z_v = jnp.split(jax.nn.gelu(z_sgu), 2, axis=-1)
    y_a = short_conv_mix(b_gate, c_gate, h, conv_w)
    y_b = spatial_gating(z_u, z_v, ln_g, ln_b, w_s, b_s)
    return jnp.concatenate([y_a, y_b], axis=-1) @ w_out


def stick_breaking_attention(q, k, v):
    seq = q.shape[2]
    qf = q.astype(jnp.float32) * (SB_HEAD_DIM ** -0.5)
    kf = k.astype(jnp.float32)
    vf = v.astype(jnp.float32)
    outs = []
    for i in range(seq // SB_QBLOCK):
        start, end = i * SB_QBLOCK, (i + 1) * SB_QBLOCK
        z = jnp.einsum("bhtd,bhsd->bhts", qf[:, :, start:end], kf[:, :, :end])
        t_pos = start + jnp.arange(SB_QBLOCK)
        s_pos = jnp.arange(end)
        strict = s_pos[None, :] < t_pos[:, None]
        log_1m = jnp.where(strict, jax.nn.log_sigmoid(-z), 0.0)
        between = lax.cumsum(log_1m, axis=3, reverse=True) - log_1m
        att = jnp.where(strict, jnp.exp(jax.nn.log_sigmoid(z) + between), 0.0)
        outs.append(jnp.einsum("bhts,bhsd->bhtd", att, vf[:, :, :end]))
    return jnp.concatenate(outs, axis=2).astype(v.dtype)


def stick_breaking_mixer(x, w_qkv, w_out):
    bsz, seq, _ = x.shape
    qkv = (x @ w_qkv).reshape(bsz, seq, 3, SB_HEADS, SB_HEAD_DIM)
    q = jnp.transpose(qkv[:, :, 0], (0, 2, 1, 3))
    k = jnp.transpose(qkv[:, :, 1], (0, 2, 1, 3))
    v = jnp.transpose(qkv[:, :, 2], (0, 2, 1, 3))
    o = stick_breaking_attention(q, k, v)
    return jnp.transpose(o, (0, 2, 1, 3)).reshape(bsz, seq, D_MODEL) @ w_out


def hierarchical_moe(x, w_group, b_group, w_router, b_router, w1, w3, w2):
    bsz, seq, d = x.shape
    xt = x.reshape(-1, d)
    g_prob = jax.nn.softmax((xt @ w_group).astype(jnp.float32) + b_group.astype(jnp.float32), axis=-1)
    g_top, g_idx = lax.top_k(g_prob, 1)
    e_logits_all = jnp.einsum("td,gde->tge", xt, w_router).astype(jnp.float32) + b_router.astype(jnp.float32)
    e_logits = jnp.take_along_axis(e_logits_all, g_idx[:, :, None], axis=1)[:, 0]
    e_top, e_idx = lax.top_k(jax.nn.softmax(e_logits, axis=-1), TOP_K_IN_GROUP)
    e_w = e_top / jnp.sum(e_top, axis=-1, keepdims=True)
    w_in_group = jnp.sum(jax.nn.one_hot(e_idx, EXPERTS_PER_GROUP) * e_w[..., None], axis=1)
    combine = jax.nn.one_hot(g_idx[:, 0], N_GROUPS)[:, :, None] * (g_top * w_in_group)[:, None, :]
    y = jnp.zeros((xt.shape[0], d), jnp.float32)
    for g in range(N_GROUPS):
        hid = jax.nn.silu(jnp.einsum("td,edf->tef", xt, w1[g])) * jnp.einsum("td,edf->tef", xt, w3[g])
        y = y + jnp.einsum("tef,efd->td", hid * combine[:, g, :, None].astype(hid.dtype), w2[g])
    return y.astype(x.dtype).reshape(bsz, seq, d)


def setup_inputs(seed: int = 0) -> dict:
    key = jax.random.key(seed)
    ks = jax.random.split(key, 24)
    f32 = jnp.float32

    def nrm(k, shape, scale):
        return jax.random.normal(k, shape, f32) * scale

    return {
        "x": nrm(ks[0], (BATCH, SEQ, D_MODEL), 1.0),
        "even_w_in": nrm(ks[1], (N_EVEN, D_MODEL, IN_COLS), D_MODEL ** -0.5),
        "even_conv_w": nrm(ks[2], (N_EVEN, CONV_K, CONV_WIDTH), CONV_K ** -0.5),
        "even_sgu_ln_g": 1.0 + nrm(ks[3], (N_EVEN, SGU_WIDTH), 0.02),
        "even_sgu_ln_b": nrm(ks[4], (N_EVEN, SGU_WIDTH), 0.02),
        "even_sgu_w_s": nrm(ks[5], (N_EVEN, SGU_HEADS, SGU_BLOCK, SGU_BLOCK), SGU_BLOCK ** -0.5),
        "even_sgu_b_s": 1.0 + nrm(ks[6], (N_EVEN, SGU_HEADS, SGU_BLOCK), 0.02),
        "even_w_out": nrm(ks[7], (N_EVEN, D_MODEL, D_MODEL), DN_BETA * D_MODEL ** -0.5),
        "odd_w_qkv": nrm(ks[8], (N_ODD, D_MODEL, 3 * D_MODEL), D_MODEL ** -0.5),
        "odd_w_out": nrm(ks[9], (N_ODD, D_MODEL, D_MODEL), DN_BETA * D_MODEL ** -0.5),
        "mix_ln_g": 1.0 + nrm(ks[10], (DEPTH, D_MODEL), 0.02),
        "mix_ln_b": nrm(ks[11], (DEPTH, D_MODEL), 0.02),
        "moe_w_group": nrm(ks[12], (DEPTH, D_MODEL, N_GROUPS), D_MODEL ** -0.5),
        "moe_b_group": nrm(ks[13], (DEPTH, N_GROUPS), 0.01),
        "moe_w_router": nrm(ks[14], (DEPTH, N_GROUPS, D_MODEL, EXPERTS_PER_GROUP), D_MODEL ** -0.5),
        "moe_b_router": nrm(ks[15], (DEPTH, N_GROUPS, EXPERTS_PER_GROUP), 0.01),
        "moe_w1": nrm(ks[16], (DEPTH, N_GROUPS, EXPERTS_PER_GROUP, D_MODEL, D_EXPERT), D_MODEL ** -0.5),
        "moe_w3": nrm(ks[17], (DEPTH, N_GROUPS, EXPERTS_PER_GROUP, D_MODEL, D_EXPERT), D_MODEL ** -0.5),
        "moe_w2": nrm(ks[18], (DEPTH, N_GROUPS, EXPERTS_PER_GROUP, D_EXPERT, D_MODEL), DN_BETA * D_EXPERT ** -0.5),
        "ffn_ln_g": 1.0 + nrm(ks[19], (DEPTH, D_MODEL), 0.02),
        "ffn_ln_b": nrm(ks[20], (DEPTH, D_MODEL), 0.02),
    }


def reference(x, even_w_in, even_conv_w, even_sgu_ln_g, even_sgu_ln_b, even_sgu_w_s,
              even_sgu_b_s, even_w_out, odd_w_qkv, odd_w_out, mix_ln_g, mix_ln_b,
              moe_w_group, moe_b_group, moe_w_router, moe_b_router, moe_w1, moe_w3,
              moe_w2, ffn_ln_g, ffn_ln_b):
    for layer in range(DEPTH):
        i = layer // 2
        if layer % 2 == 0:
            mix = conv_sgu_mixer(x, even_w_in[i], even_conv_w[i], even_sgu_ln_g[i],
                                 even_sgu_ln_b[i], even_sgu_w_s[i], even_sgu_b_s[i], even_w_out[i])
        else:
            mix = stick_breaking_mixer(x, odd_w_qkv[i], odd_w_out[i])
        x = layer_norm(DN_ALPHA * x + mix, mix_ln_g[layer], mix_ln_b[layer])
        ffn = hierarchical_moe(x, moe_w_group[layer], moe_b_group[layer], moe_w_router[layer],
                               moe_b_router[layer], moe_w1[layer], moe_w3[layer], moe_w2[layer])
        x = layer_norm(DN_ALPHA * x + ffn, ffn_ln_g[layer], ffn_ln_b[layer])
    return x
```

```python
import functools

import jax
import jax.numpy as jnp
from jax import lax
from jax.experimental import pallas as pl
from jax.experimental.pallas import tpu as pltpu

F32 = jnp.float32
BF16 = jnp.bfloat16

DEPTH = 2
DN_ALPHA = (2 * DEPTH) ** 0.25
LN_EPS = 1e-5

CONV_WIDTH = 512
CONV_K = 3
SGU_WIDTH = 512
SGU_HEADS = 8
SGU_HEAD_DIM = 64
SGU_BLOCK = 128
CHUNK = 64
SB_HEADS = 16
SB_HEAD_DIM = 64
N_GROUPS = 4
EXPERTS_PER_GROUP = 4
N_EXPERTS = N_GROUPS * EXPERTS_PER_GROUP

LANES = 128
VMEM_LIMIT = 56 * 1024 * 1024

MIX_TM = 512
PROJ_TM = 512
ROUTE_TM = 1024
MOE_TM = 1024
ATT_T = 256


def _layer_norm(r, g, b):
    mu = jnp.mean(r, axis=-1, keepdims=True)
    c = r - mu
    var = jnp.mean(c * c, axis=-1, keepdims=True)
    return c * lax.rsqrt(var + LN_EPS) * g + b


def _gelu_tanh(x):
    return 0.5 * x * (1.0 + jnp.tanh(0.7978845608028654 * (x + 0.044715 * (x * x * x))))


def _mixer0_kernel(x_ref, w_in_ref, conv_w_ref, sgu_g_ref, sgu_b_ref, ws_ref, bs_ref, w_out_ref,
                   ln_g_ref, ln_b_ref, o_ref, tail_ref, y_ref):
    tm = x_ref.shape[1]

    @pl.when(pl.program_id(1) == 0)
    def _():
        tail_ref[...] = jnp.zeros_like(tail_ref)

    x = x_ref[0]
    xb = x.astype(BF16)
    cw = CONV_WIDTH
    bch = jnp.dot(xb, w_in_ref[:, :3 * cw], preferred_element_type=F32)
    b_gate, c_gate, h = bch[:, :cw], bch[:, cw:2 * cw], bch[:, 2 * cw:]
    u = c_gate * h
    cat = jnp.concatenate([tail_ref[...], u], axis=0)
    tail_ref[...] = u[tm - 8:, :]
    u1 = cat[7:7 + tm, :]
    u2 = cat[6:6 + tm, :]
    conv = conv_w_ref[0:1, :] * u2 + conv_w_ref[1:2, :] * u1 + conv_w_ref[2:3, :] * u
    y_ref[:, :cw] = (b_gate * conv).astype(BF16)

    z = _gelu_tanh(jnp.dot(xb, w_in_ref[:, 3 * cw:], preferred_element_type=F32))
    z_u, z_v = z[:, :SGU_WIDTH], z[:, SGU_WIDTH:]
    v = _layer_norm(z_v, sgu_g_ref[...], sgu_b_ref[...])
    lane = lax.broadcasted_iota(jnp.int32, (tm, LANES), 1)
    first_head = lane < SGU_HEAD_DIM
    nblk = tm // SGU_BLOCK
    for j in range(SGU_HEADS // 2):
        vp = v[:, j * LANES:(j + 1) * LANES]
        v_a = jnp.where(first_head, vp, 0.0).astype(BF16)
        v_b = jnp.where(first_head, 0.0, vp).astype(BF16)
        rhs = jnp.concatenate(
            [jnp.concatenate([v_a[n * SGU_BLOCK:(n + 1) * SGU_BLOCK],
                              v_b[n * SGU_BLOCK:(n + 1) * SGU_BLOCK]], axis=0)
             for n in range(nblk)], axis=1)
        mixed = jnp.dot(ws_ref[j], rhs, preferred_element_type=F32)
        bias = bs_ref[:, j * LANES:(j + 1) * LANES]
        for n in range(nblk):
            rows = slice(n * SGU_BLOCK, (n + 1) * SGU_BLOCK)
            gate = z_u[rows, j * LANES:(j + 1) * LANES]
            y_ref[rows, cw + j * LANES:cw + (j + 1) * LANES] = (
                gate * (mixed[:, n * LANES:(n + 1) * LANES] + bias)).astype(BF16)

    mix = jnp.dot(y_ref[...], w_out_ref[...], preferred_element_type=F32)
    o_ref[0] = _layer_norm(DN_ALPHA * x + mix, ln_g_ref[...], ln_b_ref[...])


def _mixer0(x, w_in, conv_w, sgu_g, sgu_b, w_s, b_s, w_out, ln_g, ln_b):
    bsz, seq, d = x.shape
    tm = MIX_TM
    pos = jnp.arange(SGU_BLOCK)
    chunk_causal = (pos[None, :] // CHUNK) <= (pos[:, None] // CHUNK)
    w_masked = jnp.where(chunk_causal[None], w_s, 0.0).astype(BF16)
    ws_pairs = jnp.concatenate([w_masked[0::2], w_masked[1::2]], axis=2)
    bias = jnp.repeat(jnp.transpose(b_s), SGU_HEAD_DIM, axis=1)
    full = lambda shape: pl.BlockSpec(shape, lambda b, s: (0,) * len(shape))
    return pl.pallas_call(
        _mixer0_kernel,
        grid=(bsz, seq // tm),
        in_specs=[
            pl.BlockSpec((1, tm, d), lambda b, s: (b, s, 0)),
            full(w_in.shape), full(conv_w.shape), full((1, SGU_WIDTH)), full((1, SGU_WIDTH)),
            full(ws_pairs.shape), full(bias.shape), full(w_out.shape),
            full((1, d)), full((1, d)),
        ],
        out_specs=pl.BlockSpec((1, tm, d), lambda b, s: (b, s, 0)),
        out_shape=jax.ShapeDtypeStruct(x.shape, F32),
        scratch_shapes=[pltpu.VMEM((8, CONV_WIDTH), F32), pltpu.VMEM((tm, d), BF16)],
        compiler_params=pltpu.CompilerParams(
            dimension_semantics=("arbitrary", "arbitrary"), vmem_limit_bytes=VMEM_LIMIT),
        name="mixer0",
    )(x, w_in.astype(BF16), conv_w, sgu_g.reshape(1, -1), sgu_b.reshape(1, -1), ws_pairs, bias,
      w_out.astype(BF16), ln_g.reshape(1, -1), ln_b.reshape(1, -1))


def _router_kernel(x_ref, w_ref, b_ref, comb_ref):
    logits = lax.dot_general(w_ref[...], x_ref[...], (((1,), (1,)), ((), ())),
                             precision=lax.Precision.HIGHEST,
                             preferred_element_type=F32) + b_ref[...]
    gl = [logits[g:g + 1, :] for g in range(N_GROUPS)]
    g_max = functools.reduce(jnp.maximum, gl)
    g_top = 1.0 / functools.reduce(lambda a, b: a + b, [jnp.exp(l - g_max) for l in gl])
    g_idx = jnp.full_like(g_max, N_GROUPS).astype(jnp.int32)
    for g in reversed(range(N_GROUPS)):
        g_idx = jnp.where(gl[g] == g_max, g, g_idx)
    el = []
    for e in range(EXPERTS_PER_GROUP):
        acc = jnp.zeros_like(g_max)
        for g in range(N_GROUPS):
            r = N_GROUPS + g * EXPERTS_PER_GROUP + e
            acc = jnp.where(g_idx == g, logits[r:r + 1, :], acc)
        el.append(acc)
    m1 = functools.reduce(jnp.maximum, el)
    i1 = jnp.full_like(g_idx, EXPERTS_PER_GROUP)
    for e in reversed(range(EXPERTS_PER_GROUP)):
        i1 = jnp.where(el[e] == m1, e, i1)
    rest = [jnp.where(i1 == e, -jnp.inf, el[e]) for e in range(EXPERTS_PER_GROUP)]
    m2 = functools.reduce(jnp.maximum, rest)
    i2 = jnp.full_like(g_idx, EXPERTS_PER_GROUP)
    for e in reversed(range(EXPERTS_PER_GROUP)):
        i2 = jnp.where(rest[e] == m2, e, i2)
    p2 = jnp.exp(m2 - m1)
    w1 = g_top / (1.0 + p2)
    w2 = g_top * p2 / (1.0 + p2)
    for g in range(N_GROUPS):
        for e in range(EXPERTS_PER_GROUP):
            c = jnp.where(i1 == e, w1, jnp.where(i2 == e, w2, 0.0))
            r = g * EXPERTS_PER_GROUP + e
            comb_ref[r:r + 1, :] = jnp.where(g_idx == g, c, 0.0)


def _router(xt, w_group, b_group, w_router, b_router):
    t, d = xt.shape
    tm = ROUTE_TM
    rows = 32
    w_all = jnp.concatenate(
        [jnp.transpose(w_group), jnp.transpose(w_router, (0, 2, 1)).reshape(N_EXPERTS, d)], axis=0)
    w_all = jnp.pad(w_all, ((0, rows - w_all.shape[0]), (0, 0)))
    b_all = jnp.pad(jnp.concatenate([b_group, b_router.reshape(-1)]), (0, rows - N_GROUPS - N_EXPERTS))
    return pl.pallas_call(
        _router_kernel,
        grid=(t // tm,),
        in_specs=[pl.BlockSpec((tm, d), lambda i: (i, 0)),
                  pl.BlockSpec((rows, d), lambda i: (0, 0)),
                  pl.BlockSpec((rows, 1), lambda i: (0, 0))],
        out_specs=pl.BlockSpec((N_EXPERTS, tm), lambda i: (0, i)),
        out_shape=jax.ShapeDtypeStruct((N_EXPERTS, t), F32),
        compiler_params=pltpu.CompilerParams(
            dimension_semantics=("arbitrary",), vmem_limit_bytes=VMEM_LIMIT),
        name="router",
    )(xt, w_all, b_all.reshape(rows, 1))


def _moe_kernel(x_ref, comb_ref, w1_ref, w3_ref, w2_ref, ln_g_ref, ln_b_ref, o_ref, acc_ref, xb_ref):
    j = pl.program_id(1)

    @pl.when(j == 0)
    def _():
        xb_ref[...] = x_ref[...].astype(BF16)
        acc_ref[...] = jnp.zeros_like(acc_ref)

    xb = xb_ref[...]
    h1 = jnp.dot(xb, w1_ref[0], preferred_element_type=F32)
    h3 = jnp.dot(xb, w3_ref[0], preferred_element_type=F32)
    lane = lax.broadcasted_iota(jnp.int32, comb_ref.shape, 1)
    c = jnp.sum(jnp.where(lane == j, comb_ref[...], 0.0), axis=1, keepdims=True)
    hid = (h1 * jax.nn.sigmoid(h1)) * h3 * c
    acc_ref[...] += jnp.dot(hid.astype(BF16), w2_ref[0], preferred_element_type=F32)

    @pl.when(j == pl.num_programs(1) - 1)
    def _():
        o_ref[...] = _layer_norm(DN_ALPHA * x_ref[...] + acc_ref[...], ln_g_ref[...], ln_b_ref[...])


def _moe(xt, comb, w1, w3, w2, ln_g, ln_b):
    t, d = xt.shape
    f = w1.shape[-1]
    tm = MOE_TM
    return pl.pallas_call(
        _moe_kernel,
        grid=(t // tm, N_EXPERTS),
        in_specs=[pl.BlockSpec((tm, d), lambda i, j: (i, 0)),
                  pl.BlockSpec((tm, N_EXPERTS), lambda i, j: (i, 0)),
                  pl.BlockSpec((1, d, f), lambda i, j: (j, 0, 0)),
                  pl.BlockSpec((1, d, f), lambda i, j: (j, 0, 0)),
                  pl.BlockSpec((1, f, d), lambda i, j: (j, 0, 0)),
                  pl.BlockSpec((1, d), lambda i, j: (0, 0)),
                  pl.BlockSpec((1, d), lambda i, j: (0, 0))],
        out_specs=pl.BlockSpec((tm, d), lambda i, j: (i, 0)),
        out_shape=jax.ShapeDtypeStruct((t, d), F32),
        scratch_shapes=[pltpu.VMEM((tm, d), F32), pltpu.VMEM((tm, d), BF16)],
        compiler_params=pltpu.CompilerParams(
            dimension_semantics=("arbitrary", "arbitrary"), vmem_limit_bytes=VMEM_LIMIT),
        name="moe_ffn",
    )(xt, comb, w1.reshape(N_EXPERTS, d, f).astype(BF16), w3.reshape(N_EXPERTS, d, f).astype(BF16),
      w2.reshape(N_EXPERTS, f, d).astype(BF16), ln_g.reshape(1, -1), ln_b.reshape(1, -1))


def _moe_layer(x, w_group, b_group, w_router, b_router, w1, w3, w2, ln_g, ln_b):
    bsz, seq, d = x.shape
    xt = x.reshape(-1, d)
    comb = jnp.transpose(_router(xt, w_group, b_group, w_router, b_router))
    return _moe(xt, comb, w1, w3, w2, ln_g, ln_b).reshape(bsz, seq, d)


def _qkv_kernel(x_ref, w_ref, o_ref):
    d = x_ref.shape[1]
    xb = x_ref[...].astype(BF16)
    for part in range(3):
        y = jnp.dot(xb, w_ref[:, part * d:(part + 1) * d], preferred_element_type=F32)
        if part == 0:
            y = y * (SB_HEAD_DIM ** -0.5)
        o_ref[:, part * d:(part + 1) * d] = y.astype(BF16)


def _qkv(xt, w_qkv):
    t, d = xt.shape
    tm = PROJ_TM
    return pl.pallas_call(
        _qkv_kernel,
        grid=(t // tm,),
        in_specs=[pl.BlockSpec((tm, d), lambda i: (i, 0)),
                  pl.BlockSpec((d, 3 * d), lambda i: (0, 0))],
        out_specs=pl.BlockSpec((tm, 3 * d), lambda i: (i, 0)),
        out_shape=jax.ShapeDtypeStruct((t, 3 * d), BF16),
        compiler_params=pltpu.CompilerParams(
            dimension_semantics=("arbitrary",), vmem_limit_bytes=VMEM_LIMIT),
        name="qkv_proj",
    )(xt, w_qkv.astype(BF16))


def _softplus(z):
    return jnp.maximum(z, 0.0) + jnp.log(1.0 + jnp.exp(-jnp.abs(z)))


def _attention_kernel(q_ref, k_ref, v_ref, o_ref):
    t = q_ref.shape[1]
    i = pl.program_id(2)
    q = q_ref[0]
    lane = lax.broadcasted_iota(jnp.int32, (t, LANES), 1)
    first_head = lane < SB_HEAD_DIM
    q_heads = (jnp.where(first_head, q, jnp.zeros_like(q)), jnp.where(first_head, jnp.zeros_like(q), q))
    row = lax.broadcasted_iota(jnp.int32, (t, t), 0)
    col = lax.broadcasted_iota(jnp.int32, (t, t), 1)
    tri = jnp.where(row >= col, 1.0, 0.0).astype(BF16)
    strict = col < row

    def block(kb, carry, diagonal):
        acc, r0, r1 = carry
        start = pl.multiple_of(kb * t, t)
        k = k_ref[0, pl.ds(start, t), :]
        v = v_ref[0, pl.ds(start, t), :]
        outs, rs = [], []
        for qh, r in zip(q_heads, (r0, r1)):
            z = lax.dot_general(qh, k, (((1,), (1,)), ((), ())), preferred_element_type=F32)
            sp = _softplus(z)
            if diagonal:
                sp = jnp.where(strict, sp, 0.0)
            hi = sp.astype(BF16)
            lo = (sp - hi.astype(F32)).astype(BF16)
            csum = (jnp.dot(hi, tri, preferred_element_type=F32)
                    + jnp.dot(lo, tri, preferred_element_type=F32))
            att = jnp.exp(z - csum - r)
            if diagonal:
                att = jnp.where(strict, att, 0.0)
            outs.append(jnp.dot(att.astype(BF16), v, preferred_element_type=F32))
            rs.append(r + csum[:, 0:1])
        acc = acc + jnp.where(first_head, outs[0], outs[1])
        return acc, rs[0], rs[1]

    zero_r = jnp.zeros((t, 1), F32)
    carry = block(i, (jnp.zeros((t, LANES), F32), zero_r, zero_r), True)
    carry = lax.fori_loop(0, i, lambda it, c: block(i - 1 - it, c, False), carry)
    o_ref[0] = carry[0].astype(o_ref.dtype)


def _attention(qkv, bsz, seq):
    d = qkv.shape[-1] // 3
    qkv = qkv.reshape(bsz, seq, 3 * d)
    t = ATT_T
    npair = d // LANES
    return pl.pallas_call(
        _attention_kernel,
        grid=(bsz, npair, seq // t),
        in_specs=[pl.BlockSpec((1, t, LANES), lambda b, p, i: (b, i, p)),
                  pl.BlockSpec((1, seq, LANES), lambda b, p, i: (b, 0, npair + p)),
                  pl.BlockSpec((1, seq, LANES), lambda b, p, i: (b, 0, 2 * npair + p))],
        out_specs=pl.BlockSpec((1, t, LANES), lambda b, p, i: (b, i, p)),
        out_shape=jax.ShapeDtypeStruct((bsz, seq, d), BF16),
        compiler_params=pltpu.CompilerParams(
            dimension_semantics=("arbitrary", "arbitrary", "arbitrary"), vmem_limit_bytes=VMEM_LIMIT),
        name="sb_attention",
    )(qkv, qkv, qkv)


def _proj_ln_kernel(y_ref, x_ref, w_ref, ln_g_ref, ln_b_ref, o_ref):
    mix = jnp.dot(y_ref[...], w_ref[...], preferred_element_type=F32)
    o_ref[...] = _layer_norm(DN_ALPHA * x_ref[...] + mix, ln_g_ref[...], ln_b_ref[...])


def _proj_ln(y, xt, w, ln_g, ln_b):
    t, d = xt.shape
    tm = PROJ_TM
    return pl.pallas_call(
        _proj_ln_kernel,
        grid=(t // tm,),
        in_specs=[pl.BlockSpec((tm, d), lambda i: (i, 0)),
                  pl.BlockSpec((tm, d), lambda i: (i, 0)),
                  pl.BlockSpec((d, d), lambda i: (0, 0)),
                  pl.BlockSpec((1, d), lambda i: (0, 0)),
                  pl.BlockSpec((1, d), lambda i: (0, 0))],
        out_specs=pl.BlockSpec((tm, d), lambda i: (i, 0)),
        out_shape=jax.ShapeDtypeStruct((t, d), F32),
        compiler_params=pltpu.CompilerParams(
            dimension_semantics=("arbitrary",), vmem_limit_bytes=VMEM_LIMIT),
        name="out_proj_ln",
    )(y, xt, w.astype(BF16), ln_g.reshape(1, -1), ln_b.reshape(1, -1))


def _sb_mixer(x, w_qkv, w_out, ln_g, ln_b):
    bsz, seq, d = x.shape
    xt = x.reshape(-1, d)
    o = _attention(_qkv(xt, w_qkv), bsz, seq)
    return _proj_ln(o.reshape(-1, d), xt, w_out, ln_g, ln_b).reshape(bsz, seq, d)


def kernel(x, even_w_in, even_conv_w, even_sgu_ln_g, even_sgu_ln_b, even_sgu_w_s, even_sgu_b_s, even_w_out, odd_w_qkv, odd_w_out, mix_ln_g, mix_ln_b, moe_w_group, moe_b_group, moe_w_router, moe_b_router, moe_w1, moe_w3, moe_w2, ffn_ln_g, ffn_ln_b):
    for layer in range(DEPTH):
        i = layer // 2
        if layer % 2 == 0:
            x = _mixer0(x, even_w_in[i], even_conv_w[i], even_sgu_ln_g[i], even_sgu_ln_b[i],
                        even_sgu_w_s[i], even_sgu_b_s[i], even_w_out[i], mix_ln_g[layer], mix_ln_b[layer])
        else:
            x = _sb_mixer(x, odd_w_qkv[i], odd_w_out[i], mix_ln_g[layer], mix_ln_b[layer])
        x = _moe_layer(x, moe_w_group[layer], moe_b_group[layer], moe_w_router[layer],
                       moe_b_router[layer], moe_w1[layer], moe_w3[layer], moe_w2[layer],
                       ffn_ln_g[layer], ffn_ln_b[layer])
    return x
```

```python
import functools

import jax
import jax.numpy as jnp
from jax import lax
from jax.experimental import pallas as pl
from jax.experimental.pallas import tpu as pltpu

F32 = jnp.float32
BF16 = jnp.bfloat16

DEPTH = 2
DN_ALPHA = (2 * DEPTH) ** 0.25
LN_EPS = 1e-5

CONV_WIDTH = 512
CONV_K = 3
SGU_WIDTH = 512
SGU_HEADS = 8
SGU_HEAD_DIM = 64
SGU_BLOCK = 128
CHUNK = 64
SB_HEADS = 16
SB_HEAD_DIM = 64
N_GROUPS = 4
EXPERTS_PER_GROUP = 4
N_EXPERTS = N_GROUPS * EXPERTS_PER_GROUP

LANES = 128
VMEM_LIMIT = 56 * 1024 * 1024

MIX_TM = 512
PROJ_TM = 512
ROUTE_TM = 1024
MOE_TM = 1024
ATT_T = 256
LOG2_E = 1.4426950408889634
ATT_DEAD = 150.0


def _layer_norm(r, g, b):
    mu = jnp.mean(r, axis=-1, keepdims=True)
    c = r - mu
    var = jnp.mean(c * c, axis=-1, keepdims=True)
    return c * lax.rsqrt(var + LN_EPS) * g + b


def _gelu_tanh(x):
    return 0.5 * x * (1.0 + jnp.tanh(0.7978845608028654 * (x + 0.044715 * (x * x * x))))


def _mixer0_kernel(x_ref, w_in_ref, conv_w_ref, sgu_g_ref, sgu_b_ref, ws_ref, bs_ref, w_out_ref,
                   ln_g_ref, ln_b_ref, o_ref, tail_ref, y_ref):
    tm = x_ref.shape[1]

    @pl.when(pl.program_id(1) == 0)
    def _():
        tail_ref[...] = jnp.zeros_like(tail_ref)

    x = x_ref[0]
    xb = x.astype(BF16)
    cw = CONV_WIDTH
    bch = jnp.dot(xb, w_in_ref[:, :3 * cw], preferred_element_type=F32)
    b_gate, c_gate, h = bch[:, :cw], bch[:, cw:2 * cw], bch[:, 2 * cw:]
    u = c_gate * h
    cat = jnp.concatenate([tail_ref[...], u], axis=0)
    tail_ref[...] = u[tm - 8:, :]
    u1 = cat[7:7 + tm, :]
    u2 = cat[6:6 + tm, :]
    conv = conv_w_ref[0:1, :] * u2 + conv_w_ref[1:2, :] * u1 + conv_w_ref[2:3, :] * u
    y_ref[:, :cw] = (b_gate * conv).astype(BF16)

    z = _gelu_tanh(jnp.dot(xb, w_in_ref[:, 3 * cw:], preferred_element_type=F32))
    z_u, z_v = z[:, :SGU_WIDTH], z[:, SGU_WIDTH:]
    v = _layer_norm(z_v, sgu_g_ref[...], sgu_b_ref[...])
    lane = lax.broadcasted_iota(jnp.int32, (tm, LANES), 1)
    first_head = lane < SGU_HEAD_DIM
    nblk = tm // SGU_BLOCK
    for j in range(SGU_HEADS // 2):
        vp = v[:, j * LANES:(j + 1) * LANES]
        v_a = jnp.where(first_head, vp, 0.0).astype(BF16)
        v_b = jnp.where(first_head, 0.0, vp).astype(BF16)
        rhs = jnp.concatenate(
            [jnp.concatenate([v_a[n * SGU_BLOCK:(n + 1) * SGU_BLOCK],
                              v_b[n * SGU_BLOCK:(n + 1) * SGU_BLOCK]], axis=0)
             for n in range(nblk)], axis=1)
        mixed = jnp.dot(ws_ref[j], rhs, preferred_element_type=F32)
        bias = bs_ref[:, j * LANES:(j + 1) * LANES]
        for n in range(nblk):
            rows = slice(n * SGU_BLOCK, (n + 1) * SGU_BLOCK)
            gate = z_u[rows, j * LANES:(j + 1) * LANES]
            y_ref[rows, cw + j * LANES:cw + (j + 1) * LANES] = (
                gate * (mixed[:, n * LANES:(n + 1) * LANES] + bias)).astype(BF16)

    mix = jnp.dot(y_ref[...], w_out_ref[...], preferred_element_type=F32)
    o_ref[0] = _layer_norm(DN_ALPHA * x + mix, ln_g_ref[...], ln_b_ref[...])


def _mixer0(x, w_in, conv_w, sgu_g, sgu_b, w_s, b_s, w_out, ln_g, ln_b):
    bsz, seq, d = x.shape
    tm = MIX_TM
    pos = jnp.arange(SGU_BLOCK)
    chunk_causal = (pos[None, :] // CHUNK) <= (pos[:, None] // CHUNK)
    w_masked = jnp.where(chunk_causal[None], w_s, 0.0).astype(BF16)
    ws_pairs = jnp.concatenate([w_masked[0::2], w_masked[1::2]], axis=2)
    bias = jnp.repeat(jnp.transpose(b_s), SGU_HEAD_DIM, axis=1)
    full = lambda shape: pl.BlockSpec(shape, lambda b, s: (0,) * len(shape))
    return pl.pallas_call(
        _mixer0_kernel,
        grid=(bsz, seq // tm),
        in_specs=[
            pl.BlockSpec((1, tm, d), lambda b, s: (b, s, 0)),
            full(w_in.shape), full(conv_w.shape), full((1, SGU_WIDTH)), full((1, SGU_WIDTH)),
            full(ws_pairs.shape), full(bias.shape), full(w_out.shape),
            full((1, d)), full((1, d)),
        ],
        out_specs=pl.BlockSpec((1, tm, d), lambda b, s: (b, s, 0)),
        out_shape=jax.ShapeDtypeStruct(x.shape, F32),
        scratch_shapes=[pltpu.VMEM((8, CONV_WIDTH), F32), pltpu.VMEM((tm, d), BF16)],
        compiler_params=pltpu.CompilerParams(
            dimension_semantics=("arbitrary", "arbitrary"), vmem_limit_bytes=VMEM_LIMIT),
        name="mixer0",
    )(x, w_in.astype(BF16), conv_w, sgu_g.reshape(1, -1), sgu_b.reshape(1, -1), ws_pairs, bias,
      w_out.astype(BF16), ln_g.reshape(1, -1), ln_b.reshape(1, -1))


def _router_kernel(x_ref, w_ref, b_ref, comb_ref):
    logits = lax.dot_general(w_ref[...], x_ref[...], (((1,), (1,)), ((), ())),
                             precision=lax.Precision.HIGHEST,
                             preferred_element_type=F32) + b_ref[...]
    gl = [logits[g:g + 1, :] for g in range(N_GROUPS)]
    g_max = functools.reduce(jnp.maximum, gl)
    g_top = 1.0 / functools.reduce(lambda a, b: a + b, [jnp.exp(l - g_max) for l in gl])
    g_idx = jnp.full_like(g_max, N_GROUPS).astype(jnp.int32)
    for g in reversed(range(N_GROUPS)):
        g_idx = jnp.where(gl[g] == g_max, g, g_idx)
    el = []
    for e in range(EXPERTS_PER_GROUP):
        acc = jnp.zeros_like(g_max)
        for g in range(N_GROUPS):
            r = N_GROUPS + g * EXPERTS_PER_GROUP + e
            acc = jnp.where(g_idx == g, logits[r:r + 1, :], acc)
        el.append(acc)
    m1 = functools.reduce(jnp.maximum, el)
    i1 = jnp.full_like(g_idx, EXPERTS_PER_GROUP)
    for e in reversed(range(EXPERTS_PER_GROUP)):
        i1 = jnp.where(el[e] == m1, e, i1)
    rest = [jnp.where(i1 == e, -jnp.inf, el[e]) for e in range(EXPERTS_PER_GROUP)]
    m2 = functools.reduce(jnp.maximum, rest)
    i2 = jnp.full_like(g_idx, EXPERTS_PER_GROUP)
    for e in reversed(range(EXPERTS_PER_GROUP)):
        i2 = jnp.where(rest[e] == m2, e, i2)
    p2 = jnp.exp(m2 - m1)
    w1 = g_top / (1.0 + p2)
    w2 = g_top * p2 / (1.0 + p2)
    for g in range(N_GROUPS):
        for e in range(EXPERTS_PER_GROUP):
            c = jnp.where(i1 == e, w1, jnp.where(i2 == e, w2, 0.0))
            r = g * EXPERTS_PER_GROUP + e
            comb_ref[r:r + 1, :] = jnp.where(g_idx == g, c, 0.0)


def _router(xt, w_group, b_group, w_router, b_router):
    t, d = xt.shape
    tm = ROUTE_TM
    rows = 32
    w_all = jnp.concatenate(
        [jnp.transpose(w_group), jnp.transpose(w_router, (0, 2, 1)).reshape(N_EXPERTS, d)], axis=0)
    w_all = jnp.pad(w_all, ((0, rows - w_all.shape[0]), (0, 0)))
    b_all = jnp.pad(jnp.concatenate([b_group, b_router.reshape(-1)]), (0, rows - N_GROUPS - N_EXPERTS))
    return pl.pallas_call(
        _router_kernel,
        grid=(t // tm,),
        in_specs=[pl.BlockSpec((tm, d), lambda i: (i, 0)),
                  pl.BlockSpec((rows, d), lambda i: (0, 0)),
                  pl.BlockSpec((rows, 1), lambda i: (0, 0))],
        out_specs=pl.BlockSpec((N_EXPERTS, tm), lambda i: (0, i)),
        out_shape=jax.ShapeDtypeStruct((N_EXPERTS, t), F32),
        compiler_params=pltpu.CompilerParams(
            dimension_semantics=("arbitrary",), vmem_limit_bytes=VMEM_LIMIT),
        name="router",
    )(xt, w_all, b_all.reshape(rows, 1))


def _moe_kernel(x_ref, comb_ref, w1_ref, w3_ref, w2_ref, ln_g_ref, ln_b_ref, o_ref, acc_ref, xb_ref):
    j = pl.program_id(1)

    @pl.when(j == 0)
    def _():
        xb_ref[...] = x_ref[...].astype(BF16)
        acc_ref[...] = jnp.zeros_like(acc_ref)

    xb = xb_ref[...]
    h1 = jnp.dot(xb, w1_ref[0], preferred_element_type=F32)
    h3 = jnp.dot(xb, w3_ref[0], preferred_element_type=F32)
    lane = lax.broadcasted_iota(jnp.int32, comb_ref.shape, 1)
    c = jnp.sum(jnp.where(lane == j, comb_ref[...], 0.0), axis=1, keepdims=True)
    hid = (h1 * jax.nn.sigmoid(h1)) * h3 * c
    acc_ref[...] += jnp.dot(hid.astype(BF16), w2_ref[0], preferred_element_type=F32)

    @pl.when(j == pl.num_programs(1) - 1)
    def _():
        o_ref[...] = _layer_norm(DN_ALPHA * x_ref[...] + acc_ref[...], ln_g_ref[...], ln_b_ref[...])


def _moe(xt, comb, w1, w3, w2, ln_g, ln_b):
    t, d = xt.shape
    f = w1.shape[-1]
    tm = MOE_TM
    return pl.pallas_call(
        _moe_kernel,
        grid=(t // tm, N_EXPERTS),
        in_specs=[pl.BlockSpec((tm, d), lambda i, j: (i, 0)),
                  pl.BlockSpec((tm, N_EXPERTS), lambda i, j: (i, 0)),
                  pl.BlockSpec((1, d, f), lambda i, j: (j, 0, 0)),
                  pl.BlockSpec((1, d, f), lambda i, j: (j, 0, 0)),
                  pl.BlockSpec((1, f, d), lambda i, j: (j, 0, 0)),
                  pl.BlockSpec((1, d), lambda i, j: (0, 0)),
                  pl.BlockSpec((1, d), lambda i, j: (0, 0))],
        out_specs=pl.BlockSpec((tm, d), lambda i, j: (i, 0)),
        out_shape=jax.ShapeDtypeStruct((t, d), F32),
        scratch_shapes=[pltpu.VMEM((tm, d), F32), pltpu.VMEM((tm, d), BF16)],
        compiler_params=pltpu.CompilerParams(
            dimension_semantics=("arbitrary", "arbitrary"), vmem_limit_bytes=VMEM_LIMIT),
        name="moe_ffn",
    )(xt, comb, w1.reshape(N_EXPERTS, d, f).astype(BF16), w3.reshape(N_EXPERTS, d, f).astype(BF16),
      w2.reshape(N_EXPERTS, f, d).astype(BF16), ln_g.reshape(1, -1), ln_b.reshape(1, -1))


def _moe_layer(x, w_group, b_group, w_router, b_router, w1, w3, w2, ln_g, ln_b):
    bsz, seq, d = x.shape
    xt = x.reshape(-1, d)
    comb = jnp.transpose(_router(xt, w_group, b_group, w_router, b_router))
    return _moe(xt, comb, w1, w3, w2, ln_g, ln_b).reshape(bsz, seq, d)


def _qkv_kernel(x_ref, w_ref, o_ref):
    d = x_ref.shape[1]
    xb = x_ref[...].astype(BF16)
    for part in range(3):
        y = jnp.dot(xb, w_ref[:, part * d:(part + 1) * d], preferred_element_type=F32)
        if part == 0:
            y = y * (SB_HEAD_DIM ** -0.5 * LOG2_E)
        o_ref[:, part * d:(part + 1) * d] = y.astype(BF16)


def _qkv(xt, w_qkv):
    t, d = xt.shape
    tm = PROJ_TM
    return pl.pallas_call(
        _qkv_kernel,
        grid=(t // tm,),
        in_specs=[pl.BlockSpec((tm, d), lambda i: (i, 0)),
                  pl.BlockSpec((d, 3 * d), lambda i: (0, 0))],
        out_specs=pl.BlockSpec((tm, 3 * d), lambda i: (i, 0)),
        out_shape=jax.ShapeDtypeStruct((t, 3 * d), BF16),
        compiler_params=pltpu.CompilerParams(
            dimension_semantics=("arbitrary",), vmem_limit_bytes=VMEM_LIMIT),
        name="qkv_proj",
    )(xt, w_qkv.astype(BF16))


def _softplus2(z2):
    return jnp.maximum(z2, 0.0) + jnp.log2(1.0 + jnp.exp2(-jnp.abs(z2)))


def _attention_kernel(q_ref, k_ref, v_ref, o_ref):
    t = q_ref.shape[1]
    i = pl.program_id(2)
    q = q_ref[0]
    lane = lax.broadcasted_iota(jnp.int32, (t, LANES), 1)
    first_head = lane < SB_HEAD_DIM
    q_heads = (jnp.where(first_head, q, jnp.zeros_like(q)), jnp.where(first_head, jnp.zeros_like(q), q))
    row = lax.broadcasted_iota(jnp.int32, (t, t), 0)
    col = lax.broadcasted_iota(jnp.int32, (t, t), 1)
    tri = jnp.where(row >= col, 1.0, 0.0).astype(BF16)
    strict = col < row

    def block(kb, acc, r0, r1, diagonal):
        start = pl.multiple_of(kb * t, t)
        k = k_ref[0, pl.ds(start, t), :]
        v = v_ref[0, pl.ds(start, t), :]
        rs = (r0, r1)
        zs = [lax.dot_general(qh, k, (((1,), (1,)), ((), ())), preferred_element_type=F32)
              for qh in q_heads]
        cs, outs = [None, None], [None, None]

        def sp_of(h):
            sp = _softplus2(zs[h])
            if diagonal:
                sp = jnp.where(strict, sp, 0.0)
            return sp.astype(BF16)

        def att_of(h):
            att = jnp.exp2(zs[h] - cs[h] - rs[h])
            if diagonal:
                att = jnp.where(strict, att, 0.0)
            return att.astype(BF16)

        sp0 = sp_of(0)
        cs[0] = jnp.dot(sp0, tri, preferred_element_type=F32)
        sp1 = sp_of(1)
        cs[1] = jnp.dot(sp1, tri, preferred_element_type=F32)
        r0n = r0 + cs[0][:, 0:1]
        r1n = r1 + cs[1][:, 0:1]
        r_min = jnp.min(jnp.minimum(r0n, r1n))
        att0 = att_of(0)
        outs[0] = jnp.dot(att0, v, preferred_element_type=F32)
        att1 = att_of(1)
        outs[1] = jnp.dot(att1, v, preferred_element_type=F32)
        acc = acc + jnp.where(first_head, outs[0], outs[1])
        return acc, r0n, r1n, r_min

    zero_r = jnp.zeros((t, 1), F32)
    acc, r0, r1, r_min = block(i, jnp.zeros((t, LANES), F32), zero_r, zero_r, True)

    def cond(c):
        return jnp.logical_and(c[0] >= 0, c[4] < ATT_DEAD)

    def body(c):
        kb, acc, r0, r1, _ = c
        acc, r0, r1, r_min = block(kb, acc, r0, r1, False)
        return kb - 1, acc, r0, r1, r_min

    carry = lax.while_loop(cond, body, (i - 1, acc, r0, r1, r_min))
    o_ref[0] = carry[1].astype(o_ref.dtype)


def _attention(qkv, bsz, seq):
    d = qkv.shape[-1] // 3
    qkv = qkv.reshape(bsz, seq, 3 * d)
    t = ATT_T
    npair = d // LANES
    return pl.pallas_call(
        _attention_kernel,
        grid=(bsz, npair, seq // t),
        in_specs=[pl.BlockSpec((1, t, LANES), lambda b, p, i: (b, i, p)),
                  pl.BlockSpec((1, seq, LANES), lambda b, p, i: (b, 0, npair + p)),
                  pl.BlockSpec((1, seq, LANES), lambda b, p, i: (b, 0, 2 * npair + p))],
        out_specs=pl.BlockSpec((1, t, LANES), lambda b, p, i: (b, i, p)),
        out_shape=jax.ShapeDtypeStruct((bsz, seq, d), BF16),
        compiler_params=pltpu.CompilerParams(
            dimension_semantics=("arbitrary", "arbitrary", "arbitrary"), vmem_limit_bytes=VMEM_LIMIT),
        name="sb_attention",
    )(qkv, qkv, qkv)


def _proj_ln_kernel(y_ref, x_ref, w_ref, ln_g_ref, ln_b_ref, o_ref):
    mix = jnp.dot(y_ref[...], w_ref[...], preferred_element_type=F32)
    o_ref[...] = _layer_norm(DN_ALPHA * x_ref[...] + mix, ln_g_ref[...], ln_b_ref[...])


def _proj_ln(y, xt, w, ln_g, ln_b):
    t, d = xt.shape
    tm = PROJ_TM
    return pl.pallas_call(
        _proj_ln_kernel,
        grid=(t // tm,),
        in_specs=[pl.BlockSpec((tm, d), lambda i: (i, 0)),
                  pl.BlockSpec((tm, d), lambda i: (i, 0)),
                  pl.BlockSpec((d, d), lambda i: (0, 0)),
                  pl.BlockSpec((1, d), lambda i: (0, 0)),
                  pl.BlockSpec((1, d), lambda i: (0, 0))],
        out_specs=pl.BlockSpec((tm, d), lambda i: (i, 0)),
        out_shape=jax.ShapeDtypeStruct((t, d), F32),
        compiler_params=pltpu.CompilerParams(
            dimension_semantics=("arbitrary",), vmem_limit_bytes=VMEM_LIMIT),
        name="out_proj_ln",
    )(y, xt, w.astype(BF16), ln_g.reshape(1, -1), ln_b.reshape(1, -1))


def _sb_mixer(x, w_qkv, w_out, ln_g, ln_b):
    bsz, seq, d = x.shape
    xt = x.reshape(-1, d)
    o = _attention(_qkv(xt, w_qkv), bsz, seq)
    return _proj_ln(o.reshape(-1, d), xt, w_out, ln_g, ln_b).reshape(bsz, seq, d)


def kernel(x, even_w_in, even_conv_w, even_sgu_ln_g, even_sgu_ln_b, even_sgu_w_s, even_sgu_b_s, even_w_out, odd_w_qkv, odd_w_out, mix_ln_g, mix_ln_b, moe_w_group, moe_b_group, moe_w_router, moe_b_router, moe_w1, moe_w3, moe_w2, ffn_ln_g, ffn_ln_b):
    for layer in range(DEPTH):
        i = layer // 2
        if layer % 2 == 0:
            x = _mixer0(x, even_w_in[i], even_conv_w[i], even_sgu_ln_g[i], even_sgu_ln_b[i],
                        even_sgu_w_s[i], even_sgu_b_s[i], even_w_out[i], mix_ln_g[layer], mix_ln_b[layer])
        else:
            x = _sb_mixer(x, odd_w_qkv[i], odd_w_out[i], mix_ln_g[layer], mix_ln_b[layer])
        x = _moe_layer(x, moe_w_group[layer], moe_b_group[layer], moe_w_router[layer],
                       moe_b_router[layer], moe_w1[layer], moe_w3[layer], moe_w2[layer],
                       ffn_ln_g[layer], ffn_ln_b[layer])
    return x
```

```python
import functools

import jax
import jax.numpy as jnp
from jax import lax
from jax.experimental import pallas as pl
from jax.experimental.pallas import tpu as pltpu

F32 = jnp.float32
BF16 = jnp.bfloat16

DEPTH = 2
DN_ALPHA = (2 * DEPTH) ** 0.25
LN_EPS = 1e-5

CONV_WIDTH = 512
CONV_K = 3
SGU_WIDTH = 512
SGU_HEADS = 8
SGU_HEAD_DIM = 64
SGU_BLOCK = 128
CHUNK = 64
SB_HEADS = 16
SB_HEAD_DIM = 64
N_GROUPS = 4
EXPERTS_PER_GROUP = 4
N_EXPERTS = N_GROUPS * EXPERTS_PER_GROUP

LANES = 128
VMEM_LIMIT = 56 * 1024 * 1024

MIX_TM = 512
PROJ_TM = 512
ROUTE_TM = 1024
ROUTE_ROWS = 8
PLAN_COLS = 16
MOE_TR = 256
ATT_T = 256
LOG2_E = 1.4426950408889634
ATT_DEAD = 150.0


def _layer_norm(r, g, b):
    mu = jnp.mean(r, axis=-1, keepdims=True)
    c = r - mu
    var = jnp.mean(c * c, axis=-1, keepdims=True)
    return c * lax.rsqrt(var + LN_EPS) * g + b


def _gelu_tanh(x):
    return 0.5 * x * (1.0 + jnp.tanh(0.7978845608028654 * (x + 0.044715 * (x * x * x))))


def _mixer0_kernel(x_ref, w_in_ref, conv_w_ref, sgu_g_ref, sgu_b_ref, ws_ref, bs_ref, w_out_ref,
                   ln_g_ref, ln_b_ref, o_ref, tail_ref, y_ref):
    tm = x_ref.shape[1]

    @pl.when(pl.program_id(1) == 0)
    def _():
        tail_ref[...] = jnp.zeros_like(tail_ref)

    x = x_ref[0]
    xb = x.astype(BF16)
    cw = CONV_WIDTH
    bch = jnp.dot(xb, w_in_ref[:, :3 * cw], preferred_element_type=F32)
    b_gate, c_gate, h = bch[:, :cw], bch[:, cw:2 * cw], bch[:, 2 * cw:]
    u = c_gate * h
    cat = jnp.concatenate([tail_ref[...], u], axis=0)
    tail_ref[...] = u[tm - 8:, :]
    u1 = cat[7:7 + tm, :]
    u2 = cat[6:6 + tm, :]
    conv = conv_w_ref[0:1, :] * u2 + conv_w_ref[1:2, :] * u1 + conv_w_ref[2:3, :] * u
    y_ref[:, :cw] = (b_gate * conv).astype(BF16)

    z = _gelu_tanh(jnp.dot(xb, w_in_ref[:, 3 * cw:], preferred_element_type=F32))
    z_u, z_v = z[:, :SGU_WIDTH], z[:, SGU_WIDTH:]
    v = _layer_norm(z_v, sgu_g_ref[...], sgu_b_ref[...])
    lane = lax.broadcasted_iota(jnp.int32, (tm, LANES), 1)
    first_head = lane < SGU_HEAD_DIM
    nblk = tm // SGU_BLOCK
    for j in range(SGU_HEADS // 2):
        vp = v[:, j * LANES:(j + 1) * LANES]
        v_a = jnp.where(first_head, vp, 0.0).astype(BF16)
        v_b = jnp.where(first_head, 0.0, vp).astype(BF16)
        rhs = jnp.concatenate(
            [jnp.concatenate([v_a[n * SGU_BLOCK:(n + 1) * SGU_BLOCK],
                              v_b[n * SGU_BLOCK:(n + 1) * SGU_BLOCK]], axis=0)
             for n in range(nblk)], axis=1)
        mixed = jnp.dot(ws_ref[j], rhs, preferred_element_type=F32)
        bias = bs_ref[:, j * LANES:(j + 1) * LANES]
        for n in range(nblk):
            rows = slice(n * SGU_BLOCK, (n + 1) * SGU_BLOCK)
            gate = z_u[rows, j * LANES:(j + 1) * LANES]
            y_ref[rows, cw + j * LANES:cw + (j + 1) * LANES] = (
                gate * (mixed[:, n * LANES:(n + 1) * LANES] + bias)).astype(BF16)

    mix = jnp.dot(y_ref[...], w_out_ref[...], preferred_element_type=F32)
    o_ref[0] = _layer_norm(DN_ALPHA * x + mix, ln_g_ref[...], ln_b_ref[...])


def _mixer0(x, w_in, conv_w, sgu_g, sgu_b, w_s, b_s, w_out, ln_g, ln_b):
    bsz, seq, d = x.shape
    tm = MIX_TM
    pos = jnp.arange(SGU_BLOCK)
    chunk_causal = (pos[None, :] // CHUNK) <= (pos[:, None] // CHUNK)
    w_masked = jnp.where(chunk_causal[None], w_s, 0.0).astype(BF16)
    ws_pairs = jnp.concatenate([w_masked[0::2], w_masked[1::2]], axis=2)
    bias = jnp.repeat(jnp.transpose(b_s), SGU_HEAD_DIM, axis=1)
    full = lambda shape: pl.BlockSpec(shape, lambda b, s: (0,) * len(shape))
    return pl.pallas_call(
        _mixer0_kernel,
        grid=(bsz, seq // tm),
        in_specs=[
            pl.BlockSpec((1, tm, d), lambda b, s: (b, s, 0)),
            full(w_in.shape), full(conv_w.shape), full((1, SGU_WIDTH)), full((1, SGU_WIDTH)),
            full(ws_pairs.shape), full(bias.shape), full(w_out.shape),
            full((1, d)), full((1, d)),
        ],
        out_specs=pl.BlockSpec((1, tm, d), lambda b, s: (b, s, 0)),
        out_shape=jax.ShapeDtypeStruct(x.shape, F32),
        scratch_shapes=[pltpu.VMEM((8, CONV_WIDTH), F32), pltpu.VMEM((tm, d), BF16)],
        compiler_params=pltpu.CompilerParams(
            dimension_semantics=("arbitrary", "arbitrary"), vmem_limit_bytes=VMEM_LIMIT),
        name="mixer0",
    )(x, w_in.astype(BF16), conv_w, sgu_g.reshape(1, -1), sgu_b.reshape(1, -1), ws_pairs, bias,
      w_out.astype(BF16), ln_g.reshape(1, -1), ln_b.reshape(1, -1))


def _router_kernel(x_ref, w_ref, b_ref, route_ref):
    logits = lax.dot_general(w_ref[...], x_ref[...], (((1,), (1,)), ((), ())),
                             precision=lax.Precision.HIGHEST,
                             preferred_element_type=F32) + b_ref[...]
    gl = [logits[g:g + 1, :] for g in range(N_GROUPS)]
    g_max = functools.reduce(jnp.maximum, gl)
    g_top = 1.0 / functools.reduce(lambda a, b: a + b, [jnp.exp(l - g_max) for l in gl])
    g_idx = jnp.full_like(g_max, N_GROUPS).astype(jnp.int32)
    for g in reversed(range(N_GROUPS)):
        g_idx = jnp.where(gl[g] == g_max, g, g_idx)
    el = []
    for e in range(EXPERTS_PER_GROUP):
        acc = jnp.zeros_like(g_max)
        for g in range(N_GROUPS):
            r = N_GROUPS + g * EXPERTS_PER_GROUP + e
            acc = jnp.where(g_idx == g, logits[r:r + 1, :], acc)
        el.append(acc)
    m1 = functools.reduce(jnp.maximum, el)
    i1 = jnp.full_like(g_idx, EXPERTS_PER_GROUP)
    for e in reversed(range(EXPERTS_PER_GROUP)):
        i1 = jnp.where(el[e] == m1, e, i1)
    rest = [jnp.where(i1 == e, -jnp.inf, el[e]) for e in range(EXPERTS_PER_GROUP)]
    m2 = functools.reduce(jnp.maximum, rest)
    i2 = jnp.full_like(g_idx, EXPERTS_PER_GROUP)
    for e in reversed(range(EXPERTS_PER_GROUP)):
        i2 = jnp.where(rest[e] == m2, e, i2)
    p2 = jnp.exp(m2 - m1)
    w1 = g_top / (1.0 + p2)
    w2 = g_top * p2 / (1.0 + p2)
    for g in range(N_GROUPS):
        route_ref[g:g + 1, :] = jnp.where(g_idx == g, 1.0, 0.0)
    for e in range(EXPERTS_PER_GROUP):
        route_ref[N_GROUPS + e:N_GROUPS + e + 1, :] = jnp.where(
            i1 == e, w1, jnp.where(i2 == e, w2, 0.0))


def _router(xt, w_group, b_group, w_router, b_router):
    t, d = xt.shape
    tm = ROUTE_TM
    rows = 32
    w_all = jnp.concatenate(
        [jnp.transpose(w_group), jnp.transpose(w_router, (0, 2, 1)).reshape(N_EXPERTS, d)], axis=0)
    w_all = jnp.pad(w_all, ((0, rows - w_all.shape[0]), (0, 0)))
    b_all = jnp.pad(jnp.concatenate([b_group, b_router.reshape(-1)]), (0, rows - N_GROUPS - N_EXPERTS))
    return pl.pallas_call(
        _router_kernel,
        grid=(t // tm,),
        in_specs=[pl.BlockSpec((tm, d), lambda i: (i, 0)),
                  pl.BlockSpec((rows, d), lambda i: (0, 0)),
                  pl.BlockSpec((rows, 1), lambda i: (0, 0))],
        out_specs=pl.BlockSpec((ROUTE_ROWS, tm), lambda i: (0, i)),
        out_shape=jax.ShapeDtypeStruct((ROUTE_ROWS, t), F32),
        compiler_params=pltpu.CompilerParams(
            dimension_semantics=("arbitrary",), vmem_limit_bytes=VMEM_LIMIT),
        name="router",
    )(xt, w_all, b_all.reshape(rows, 1))


def _plan_kernel(route_ref, ut_ref, plan_ref, dst_ref, tg_ref, cum_s, m_s, vals_s, cin_s):
    t = route_ref.shape[1]
    rt = ut_ref.shape[0]
    nk = t // rt
    nt, tr, _ = plan_ref.shape
    lane = lax.broadcasted_iota(jnp.int32, (1, rt), 1)

    def cum_body(k, carry):
        off = pl.multiple_of(k * rt, rt)
        route = route_ref[:, pl.ds(off, rt)]
        cum = jnp.dot(route.astype(BF16), ut_ref[...], preferred_element_type=F32) + carry
        comb = route[N_GROUPS:, :]
        b0 = comb.astype(BF16).astype(F32)
        b1 = (comb - b0).astype(BF16).astype(F32)
        b2 = (comb - b0 - b1).astype(BF16).astype(F32)
        tok = off + lane
        pieces = ([b0[e:e + 1] for e in range(EXPERTS_PER_GROUP)]
                  + [b1[e:e + 1] for e in range(EXPERTS_PER_GROUP)]
                  + [b2[e:e + 1] for e in range(EXPERTS_PER_GROUP)]
                  + [(tok >> 8).astype(F32), (tok & 255).astype(F32)])
        for r, piece in enumerate(pieces):
            vals_s[r:r + 1, pl.ds(off, rt)] = piece
        for r in range(len(pieces), PLAN_COLS):
            vals_s[r:r + 1, pl.ds(off, rt)] = jnp.zeros((1, rt), F32)
        for g in range(N_GROUPS):
            cum_s[g, :, pl.ds(off, rt)] = cum[g:g + 1, :]
            m_s[g, :, pl.ds(off, rt)] = route[g:g + 1, :]
            cin_s[k * N_GROUPS + g] = carry[g, 0].astype(jnp.int32)
        return cum[:, rt - 1:rt]

    total = lax.fori_loop(0, nk, cum_body, jnp.zeros((ROUTE_ROWS, 1), F32))
    tile_base = [jnp.int32(0)]
    for g in range(N_GROUPS):
        cnt = total[g, 0].astype(jnp.int32)
        cin_s[nk * N_GROUPS + g] = cnt
        tile_base.append(tile_base[-1] + (cnt + tr - 1) // tr)

    def dst_body(k, _):
        off = pl.multiple_of(k * rt, rt)
        pos = jnp.zeros((1, rt), F32)
        for g in range(N_GROUPS):
            pos = pos + m_s[g, :, pl.ds(off, rt)] * (
                cum_s[g, :, pl.ds(off, rt)] + (tile_base[g] * tr - 1).astype(F32))
        dst_ref[:, pl.ds(off, rt)] = pos.astype(jnp.int32)
        return 0

    lax.fori_loop(0, nk, dst_body, 0)

    row_iota = lax.broadcasted_iota(jnp.int32, (tr, rt), 0).astype(F32)

    def tile_body(i, k_prev):
        g = ((i >= tile_base[1]).astype(jnp.int32) + (i >= tile_base[2]).astype(jnp.int32)
             + (i >= tile_base[3]).astype(jnp.int32))
        base = jnp.where(g == 0, tile_base[0],
                         jnp.where(g == 1, tile_base[1], jnp.where(g == 2, tile_base[2], tile_base[3])))
        r0 = (i - base) * tr
        tg_ref[i] = g
        target = row_iota + (r0 + 1).astype(F32)

        k_first = lax.while_loop(
            lambda k: jnp.logical_and(k < nk, cin_s[jnp.minimum(k + 1, nk) * N_GROUPS + g] <= r0),
            lambda k: k + 1, jnp.where(i == base, 0, k_prev))

        def more(c):
            return jnp.logical_and(c[0] < nk, cin_s[c[0] * N_GROUPS + g] < r0 + tr)

        def pick(c):
            k, acc = c
            off = pl.multiple_of(k * rt, rt)
            onehot = jnp.where(cum_s[g, :, pl.ds(off, rt)] == target,
                               m_s[g, :, pl.ds(off, rt)], 0.0).astype(BF16)
            vals = vals_s[:, pl.ds(off, rt)].astype(BF16)
            return k + 1, acc + lax.dot_general(onehot, vals, (((1,), (1,)), ((), ())),
                                                preferred_element_type=F32)

        k_end, acc = lax.while_loop(more, pick, (k_first, jnp.zeros((tr, PLAN_COLS), F32)))
        plan_ref[i] = acc
        return jnp.maximum(k_end - 1, k_first)

    lax.fori_loop(0, nt, tile_body, jnp.int32(0))


def _plan(route):
    t = route.shape[1]
    rt = ROUTE_TM
    tr = MOE_TR
    nt = t // tr + N_GROUPS
    idx = jnp.arange(rt)
    ut = (idx[:, None] <= idx[None, :]).astype(BF16)
    return pl.pallas_call(
        _plan_kernel,
        in_specs=[pl.BlockSpec(memory_space=pltpu.VMEM), pl.BlockSpec(memory_space=pltpu.VMEM)],
        out_specs=[pl.BlockSpec(memory_space=pltpu.VMEM), pl.BlockSpec(memory_space=pltpu.VMEM),
                   pl.BlockSpec(memory_space=pltpu.SMEM)],
        out_shape=[jax.ShapeDtypeStruct((nt, tr, PLAN_COLS), F32),
                   jax.ShapeDtypeStruct((1, t), jnp.int32),
                   jax.ShapeDtypeStruct((nt,), jnp.int32)],
        scratch_shapes=[pltpu.VMEM((N_GROUPS, 1, t), F32), pltpu.VMEM((N_GROUPS, 1, t), F32),
                        pltpu.VMEM((PLAN_COLS, t), F32),
                        pltpu.SMEM(((t // rt + 1) * N_GROUPS,), jnp.int32)],
        compiler_params=pltpu.CompilerParams(vmem_limit_bytes=VMEM_LIMIT),
        name="moe_plan",
    )(route, ut)


def _start_row_gather(idx_ref, first, src_ref, buf, slot, sem):
    for r in range(buf.shape[1]):
        pltpu.make_async_copy(src_ref.at[pl.ds(idx_ref[first + r], 1), :],
                              buf.at[slot, pl.ds(r, 1), :], sem.at[slot]).start()


def _wait_row_gather(src_ref, buf, slot, sem):
    pltpu.make_async_copy(src_ref.at[pl.ds(0, buf.shape[1]), :], buf.at[slot], sem.at[slot]).wait()


def _ffn_kernel(tg_ref, src_ref, x_ref, plan_ref, w1_ref, w3_ref, w2_ref, ln_g_ref, ln_b_ref,
                o_ref, xbuf, sem):
    i = pl.program_id(0)
    last = pl.num_programs(0) - 1
    tr = xbuf.shape[1]
    slot = i % 2

    @pl.when(i == 0)
    def _():
        _start_row_gather(src_ref, 0, x_ref, xbuf, 0, sem)

    _wait_row_gather(x_ref, xbuf, slot, sem)
    x = xbuf[slot]
    xb = x.astype(BF16)
    _start_row_gather(src_ref, jnp.minimum(i + 1, last) * tr, x_ref, xbuf, 1 - slot, sem)
    plan = plan_ref[0]
    y = jnp.zeros(x.shape, F32)
    for e in range(EXPERTS_PER_GROUP):
        n = EXPERTS_PER_GROUP
        c = plan[:, e:e + 1] + plan[:, n + e:n + e + 1] + plan[:, 2 * n + e:2 * n + e + 1]
        h1 = jnp.dot(xb, w1_ref[0, e], preferred_element_type=F32)
        h3 = jnp.dot(xb, w3_ref[0, e], preferred_element_type=F32)
        hid = (h1 * jax.nn.sigmoid(h1)) * h3 * c
        y = y + jnp.dot(hid.astype(BF16), w2_ref[0, e], preferred_element_type=F32)
    o_ref[...] = _layer_norm(DN_ALPHA * x + y, ln_g_ref[...], ln_b_ref[...])

    @pl.when(i == last)
    def _():
        _wait_row_gather(x_ref, xbuf, 1 - slot, sem)


def _ffn(tile_group, src, xt, plan, w1, w3, w2, ln_g, ln_b):
    nt, tr, _ = plan.shape
    d, f = w1.shape[-2:]
    n = EXPERTS_PER_GROUP
    wspec = lambda shape: pl.BlockSpec((1,) + shape, lambda i, tg, src: (tg[i], 0, 0, 0))
    return pl.pallas_call(
        _ffn_kernel,
        grid_spec=pltpu.PrefetchScalarGridSpec(
            num_scalar_prefetch=2,
            grid=(nt,),
            in_specs=[pl.BlockSpec(memory_space=pl.ANY),
                      pl.BlockSpec((1, tr, PLAN_COLS), lambda i, tg, src: (i, 0, 0)),
                      wspec((n, d, f)), wspec((n, d, f)), wspec((n, f, d)),
                      pl.BlockSpec((1, d), lambda i, tg, src: (0, 0)),
                      pl.BlockSpec((1, d), lambda i, tg, src: (0, 0))],
            out_specs=pl.BlockSpec((tr, d), lambda i, tg, src: (i, 0)),
            scratch_shapes=[pltpu.VMEM((2, tr, d), F32), pltpu.SemaphoreType.DMA((2,))]),
        out_shape=jax.ShapeDtypeStruct((nt * tr, d), F32),
        compiler_params=pltpu.CompilerParams(
            dimension_semantics=("arbitrary",), vmem_limit_bytes=VMEM_LIMIT),
        name="moe_ffn",
    )(tile_group, src, xt, plan, w1.astype(BF16), w3.astype(BF16), w2.astype(BF16),
      ln_g.reshape(1, -1), ln_b.reshape(1, -1))


def _unsort_kernel(dst_ref, y_ref, o_ref, buf, sem):
    i = pl.program_id(0)
    last = pl.num_programs(0) - 1
    tm = buf.shape[1]
    slot = i % 2

    @pl.when(i == 0)
    def _():
        _start_row_gather(dst_ref, 0, y_ref, buf, 0, sem)

    _wait_row_gather(y_ref, buf, slot, sem)
    o_ref[...] = buf[slot]
    _start_row_gather(dst_ref, jnp.minimum(i + 1, last) * tm, y_ref, buf, 1 - slot, sem)

    @pl.when(i == last)
    def _():
        _wait_row_gather(y_ref, buf, 1 - slot, sem)


def _unsort(dst, y, t):
    d = y.shape[1]
    tm = MOE_TR
    return pl.pallas_call(
        _unsort_kernel,
        grid_spec=pltpu.PrefetchScalarGridSpec(
            num_scalar_prefetch=1,
            grid=(t // tm,),
            in_specs=[pl.BlockSpec(memory_space=pl.ANY)],
            out_specs=pl.BlockSpec((tm, d), lambda i, dst: (i, 0)),
            scratch_shapes=[pltpu.VMEM((2, tm, d), F32), pltpu.SemaphoreType.DMA((2,))]),
        out_shape=jax.ShapeDtypeStruct((t, d), F32),
        compiler_params=pltpu.CompilerParams(
            dimension_semantics=("arbitrary",), vmem_limit_bytes=VMEM_LIMIT),
        name="moe_unsort",
    )(dst, y)


def _moe_layer(x, w_group, b_group, w_router, b_router, w1, w3, w2, ln_g, ln_b):
    bsz, seq, d = x.shape
    xt = x.reshape(-1, d)
    route = _router(xt, w_group, b_group, w_router, b_router)
    plan, dst, tile_group = _plan(route)
    n = EXPERTS_PER_GROUP
    src = (plan[:, :, 3 * n] * 256.0 + plan[:, :, 3 * n + 1]).astype(jnp.int32).reshape(-1)
    y = _ffn(tile_group, src, xt, plan, w1, w3, w2, ln_g, ln_b)
    return _unsort(dst.reshape(-1), y, xt.shape[0]).reshape(bsz, seq, d)


def _qkv_kernel(x_ref, w_ref, o_ref):
    d = x_ref.shape[1]
    xb = x_ref[...].astype(BF16)
    for part in range(3):
        y = jnp.dot(xb, w_ref[:, part * d:(part + 1) * d], preferred_element_type=F32)
        if part == 0:
            y = y * (SB_HEAD_DIM ** -0.5 * LOG2_E)
        o_ref[:, part * d:(part + 1) * d] = y.astype(BF16)


def _qkv(xt, w_qkv):
    t, d = xt.shape
    tm = PROJ_TM
    return pl.pallas_call(
        _qkv_kernel,
        grid=(t // tm,),
        in_specs=[pl.BlockSpec((tm, d), lambda i: (i, 0)),
                  pl.BlockSpec((d, 3 * d), lambda i: (0, 0))],
        out_specs=pl.BlockSpec((tm, 3 * d), lambda i: (i, 0)),
        out_shape=jax.ShapeDtypeStruct((t, 3 * d), BF16),
        compiler_params=pltpu.CompilerParams(
            dimension_semantics=("arbitrary",), vmem_limit_bytes=VMEM_LIMIT),
        name="qkv_proj",
    )(xt, w_qkv.astype(BF16))


def _softplus2(z2):
    return jnp.maximum(z2, 0.0) + jnp.log2(1.0 + jnp.exp2(-jnp.abs(z2)))


def _attention_kernel(q_ref, k_ref, v_ref, o_ref):
    t = q_ref.shape[1]
    i = pl.program_id(2)
    q = q_ref[0]
    lane = lax.broadcasted_iota(jnp.int32, (t, LANES), 1)
    first_head = lane < SB_HEAD_DIM
    q_heads = (jnp.where(first_head, q, jnp.zeros_like(q)), jnp.where(first_head, jnp.zeros_like(q), q))
    row = lax.broadcasted_iota(jnp.int32, (t, t), 0)
    col = lax.broadcasted_iota(jnp.int32, (t, t), 1)
    tri = jnp.where(row >= col, 1.0, 0.0).astype(BF16)
    strict = col < row

    def block(kb, acc, r0, r1, diagonal):
        start = pl.multiple_of(kb * t, t)
        k = k_ref[0, pl.ds(start, t), :]
        v = v_ref[0, pl.ds(start, t), :]
        rs = (r0, r1)
        zs = [lax.dot_general(qh, k, (((1,), (1,)), ((), ())), preferred_element_type=F32)
              for qh in q_heads]
        cs, outs = [None, None], [None, None]

        def sp_of(h):
            sp = _softplus2(zs[h])
            if diagonal:
                sp = jnp.where(strict, sp, 0.0)
            return sp.astype(BF16)

        def att_of(h):
            att = jnp.exp2(zs[h] - cs[h] - rs[h])
            if diagonal:
                att = jnp.where(strict, att, 0.0)
            return att.astype(BF16)

        sp0 = sp_of(0)
        cs[0] = jnp.dot(sp0, tri, preferred_element_type=F32)
        sp1 = sp_of(1)
        cs[1] = jnp.dot(sp1, tri, preferred_element_type=F32)
        r0n = r0 + cs[0][:, 0:1]
        r1n = r1 + cs[1][:, 0:1]
        r_min = jnp.min(jnp.minimum(r0n, r1n))
        att0 = att_of(0)
        outs[0] = jnp.dot(att0, v, preferred_element_type=F32)
        att1 = att_of(1)
        outs[1] = jnp.dot(att1, v, preferred_element_type=F32)
        acc = acc + jnp.where(first_head, outs[0], outs[1])
        return acc, r0n, r1n, r_min

    zero_r = jnp.zeros((t, 1), F32)
    acc, r0, r1, r_min = block(i, jnp.zeros((t, LANES), F32), zero_r, zero_r, True)

    def cond(c):
        return jnp.logical_and(c[0] >= 0, c[4] < ATT_DEAD)

    def body(c):
        kb, acc, r0, r1, _ = c
        acc, r0, r1, r_min = block(kb, acc, r0, r1, False)
        return kb - 1, acc, r0, r1, r_min

    carry = lax.while_loop(cond, body, (i - 1, acc, r0, r1, r_min))
    o_ref[0] = carry[1].astype(o_ref.dtype)


def _attention(qkv, bsz, seq):
    d = qkv.shape[-1] // 3
    qkv = qkv.reshape(bsz, seq, 3 * d)
    t = ATT_T
    npair = d // LANES
    return pl.pallas_call(
        _attention_kernel,
        grid=(bsz, npair, seq // t),
        in_specs=[pl.BlockSpec((1, t, LANES), lambda b, p, i: (b, i, p)),
                  pl.BlockSpec((1, seq, LANES), lambda b, p, i: (b, 0, npair + p)),
                  pl.BlockSpec((1, seq, LANES), lambda b, p, i: (b, 0, 2 * npair + p))],
        out_specs=pl.BlockSpec((1, t, LANES), lambda b, p, i: (b, i, p)),
        out_shape=jax.ShapeDtypeStruct((bsz, seq, d), BF16),
        compiler_params=pltpu.CompilerParams(
            dimension_semantics=("arbitrary", "arbitrary", "arbitrary"), vmem_limit_bytes=VMEM_LIMIT),
        name="sb_attention",
    )(qkv, qkv, qkv)


def _proj_ln_kernel(y_ref, x_ref, w_ref, ln_g_ref, ln_b_ref, o_ref):
    mix = jnp.dot(y_ref[...], w_ref[...], preferred_element_type=F32)
    o_ref[...] = _layer_norm(DN_ALPHA * x_ref[...] + mix, ln_g_ref[...], ln_b_ref[...])


def _proj_ln(y, xt, w, ln_g, ln_b):
    t, d = xt.shape
    tm = PROJ_TM
    return pl.pallas_call(
        _proj_ln_kernel,
        grid=(t // tm,),
        in_specs=[pl.BlockSpec((tm, d), lambda i: (i, 0)),
                  pl.BlockSpec((tm, d), lambda i: (i, 0)),
                  pl.BlockSpec((d, d), lambda i: (0, 0)),
                  pl.BlockSpec((1, d), lambda i: (0, 0)),
                  pl.BlockSpec((1, d), lambda i: (0, 0))],
        out_specs=pl.BlockSpec((tm, d), lambda i: (i, 0)),
        out_shape=jax.ShapeDtypeStruct((t, d), F32),
        compiler_params=pltpu.CompilerParams(
            dimension_semantics=("arbitrary",), vmem_limit_bytes=VMEM_LIMIT),
        name="out_proj_ln",
    )(y, xt, w.astype(BF16), ln_g.reshape(1, -1), ln_b.reshape(1, -1))


def _sb_mixer(x, w_qkv, w_out, ln_g, ln_b):
    bsz, seq, d = x.shape
    xt = x.reshape(-1, d)
    o = _attention(_qkv(xt, w_qkv), bsz, seq)
    return _proj_ln(o.reshape(-1, d), xt, w_out, ln_g, ln_b).reshape(bsz, seq, d)


def kernel(x, even_w_in, even_conv_w, even_sgu_ln_g, even_sgu_ln_b, even_sgu_w_s, even_sgu_b_s, even_w_out, odd_w_qkv, odd_w_out, mix_ln_g, mix_ln_b, moe_w_group, moe_b_group, moe_w_router, moe_b_router, moe_w1, moe_w3, moe_w2, ffn_ln_g, ffn_ln_b):
    for layer in range(DEPTH):
        i = layer // 2
        if layer % 2 == 0:
            x = _mixer0(x, even_w_in[i], even_conv_w[i], even_sgu_ln_g[i], even_sgu_ln_b[i],
                        even_sgu_w_s[i], even_sgu_b_s[i], even_w_out[i], mix_ln_g[layer], mix_ln_b[layer])
        else:
            x = _sb_mixer(x, odd_w_qkv[i], odd_w_out[i], mix_ln_g[layer], mix_ln_b[layer])
        x = _moe_layer(x, moe_w_group[layer], moe_b_group[layer], moe_w_router[layer],
                       moe_b_router[layer], moe_w1[layer], moe_w3[layer], moe_w2[layer],
                       ffn_ln_g[layer], ffn_ln_b[layer])
    return x
```

```python
import functools

import jax
import jax.numpy as jnp
from jax import lax
from jax.experimental import pallas as pl
from jax.experimental.pallas import tpu as pltpu

F32 = jnp.float32
BF16 = jnp.bfloat16

DEPTH = 2
DN_ALPHA = (2 * DEPTH) ** 0.25
LN_EPS = 1e-5

CONV_WIDTH = 512
CONV_K = 3
SGU_WIDTH = 512
SGU_HEADS = 8
SGU_HEAD_DIM = 64
SGU_BLOCK = 128
CHUNK = 64
SB_HEADS = 16
SB_HEAD_DIM = 64
N_GROUPS = 4
EXPERTS_PER_GROUP = 4
N_EXPERTS = N_GROUPS * EXPERTS_PER_GROUP

LANES = 128
VMEM_LIMIT = 56 * 1024 * 1024

MIX_TM = 512
PROJ_TM = 512
ROUTE_TM = 1024
ROUTE_ROWS = 8
PLAN_COLS = 16
MOE_TR = 256
GATHER_SLOTS = 3
UNSORT_TM = 256
ATT_T = 256
ATT_SPLIT = 1
LOG2_E = 1.4426950408889634
ATT_DEAD = 150.0


def _layer_norm(r, g, b):
    mu = jnp.mean(r, axis=-1, keepdims=True)
    c = r - mu
    var = jnp.mean(c * c, axis=-1, keepdims=True)
    return c * lax.rsqrt(var + LN_EPS) * g + b


def _gelu_tanh(x):
    return 0.5 * x * (1.0 + jnp.tanh(0.7978845608028654 * (x + 0.044715 * (x * x * x))))


def _mixer0_kernel(x_ref, w_in_ref, conv_w_ref, sgu_g_ref, sgu_b_ref, ws_ref, bs_ref, w_out_ref,
                   ln_g_ref, ln_b_ref, o_ref, tail_ref, y_ref):
    tm = x_ref.shape[1]

    @pl.when(pl.program_id(1) == 0)
    def _():
        tail_ref[...] = jnp.zeros_like(tail_ref)

    x = x_ref[0]
    xb = x.astype(BF16)
    cw = CONV_WIDTH
    bch = jnp.dot(xb, w_in_ref[:, :3 * cw], preferred_element_type=F32)
    b_gate, c_gate, h = bch[:, :cw], bch[:, cw:2 * cw], bch[:, 2 * cw:]
    u = c_gate * h
    cat = jnp.concatenate([tail_ref[...], u], axis=0)
    tail_ref[...] = u[tm - 8:, :]
    u1 = cat[7:7 + tm, :]
    u2 = cat[6:6 + tm, :]
    conv = conv_w_ref[0:1, :] * u2 + conv_w_ref[1:2, :] * u1 + conv_w_ref[2:3, :] * u
    y_ref[:, :cw] = (b_gate * conv).astype(BF16)

    z = _gelu_tanh(jnp.dot(xb, w_in_ref[:, 3 * cw:], preferred_element_type=F32))
    z_u, z_v = z[:, :SGU_WIDTH], z[:, SGU_WIDTH:]
    v = _layer_norm(z_v, sgu_g_ref[...], sgu_b_ref[...])
    lane = lax.broadcasted_iota(jnp.int32, (tm, LANES), 1)
    first_head = lane < SGU_HEAD_DIM
    nblk = tm // SGU_BLOCK
    for j in range(SGU_HEADS // 2):
        vp = v[:, j * LANES:(j + 1) * LANES]
        v_a = jnp.where(first_head, vp, 0.0).astype(BF16)
        v_b = jnp.where(first_head, 0.0, vp).astype(BF16)
        rhs = jnp.concatenate(
            [jnp.concatenate([v_a[n * SGU_BLOCK:(n + 1) * SGU_BLOCK],
                              v_b[n * SGU_BLOCK:(n + 1) * SGU_BLOCK]], axis=0)
             for n in range(nblk)], axis=1)
        mixed = jnp.dot(ws_ref[j], rhs, preferred_element_type=F32)
        bias = bs_ref[:, j * LANES:(j + 1) * LANES]
        for n in range(nblk):
            rows = slice(n * SGU_BLOCK, (n + 1) * SGU_BLOCK)
            gate = z_u[rows, j * LANES:(j + 1) * LANES]
            y_ref[rows, cw + j * LANES:cw + (j + 1) * LANES] = (
                gate * (mixed[:, n * LANES:(n + 1) * LANES] + bias)).astype(BF16)

    mix = jnp.dot(y_ref[...], w_out_ref[...], preferred_element_type=F32)
    o_ref[0] = _layer_norm(DN_ALPHA * x + mix, ln_g_ref[...], ln_b_ref[...])


def _mixer0(x, w_in, conv_w, sgu_g, sgu_b, w_s, b_s, w_out, ln_g, ln_b):
    bsz, seq, d = x.shape
    tm = MIX_TM
    pos = jnp.arange(SGU_BLOCK)
    chunk_causal = (pos[None, :] // CHUNK) <= (pos[:, None] // CHUNK)
    w_masked = jnp.where(chunk_causal[None], w_s, 0.0).astype(BF16)
    ws_pairs = jnp.concatenate([w_masked[0::2], w_masked[1::2]], axis=2)
    bias = jnp.repeat(jnp.transpose(b_s), SGU_HEAD_DIM, axis=1)
    full = lambda shape: pl.BlockSpec(shape, lambda b, s: (0,) * len(shape))
    return pl.pallas_call(
        _mixer0_kernel,
        grid=(bsz, seq // tm),
        in_specs=[
            pl.BlockSpec((1, tm, d), lambda b, s: (b, s, 0)),
            full(w_in.shape), full(conv_w.shape), full((1, SGU_WIDTH)), full((1, SGU_WIDTH)),
            full(ws_pairs.shape), full(bias.shape), full(w_out.shape),
            full((1, d)), full((1, d)),
        ],
        out_specs=pl.BlockSpec((1, tm, d), lambda b, s: (b, s, 0)),
        out_shape=jax.ShapeDtypeStruct(x.shape, F32),
        scratch_shapes=[pltpu.VMEM((8, CONV_WIDTH), F32), pltpu.VMEM((tm, d), BF16)],
        compiler_params=pltpu.CompilerParams(
            dimension_semantics=("arbitrary", "arbitrary"), vmem_limit_bytes=VMEM_LIMIT),
        name="mixer0",
    )(x, w_in.astype(BF16), conv_w, sgu_g.reshape(1, -1), sgu_b.reshape(1, -1), ws_pairs, bias,
      w_out.astype(BF16), ln_g.reshape(1, -1), ln_b.reshape(1, -1))


def _router_kernel(x_ref, w_ref, b_ref, route_ref):
    logits = lax.dot_general(w_ref[...], x_ref[...], (((1,), (1,)), ((), ())),
                             precision=lax.Precision.HIGHEST,
                             preferred_element_type=F32) + b_ref[...]
    gl = [logits[g:g + 1, :] for g in range(N_GROUPS)]
    g_max = functools.reduce(jnp.maximum, gl)
    g_top = 1.0 / functools.reduce(lambda a, b: a + b, [jnp.exp(l - g_max) for l in gl])
    g_idx = jnp.full_like(g_max, N_GROUPS).astype(jnp.int32)
    for g in reversed(range(N_GROUPS)):
        g_idx = jnp.where(gl[g] == g_max, g, g_idx)
    el = []
    for e in range(EXPERTS_PER_GROUP):
        acc = jnp.zeros_like(g_max)
        for g in range(N_GROUPS):
            r = N_GROUPS + g * EXPERTS_PER_GROUP + e
            acc = jnp.where(g_idx == g, logits[r:r + 1, :], acc)
        el.append(acc)
    m1 = functools.reduce(jnp.maximum, el)
    i1 = jnp.full_like(g_idx, EXPERTS_PER_GROUP)
    for e in reversed(range(EXPERTS_PER_GROUP)):
        i1 = jnp.where(el[e] == m1, e, i1)
    rest = [jnp.where(i1 == e, -jnp.inf, el[e]) for e in range(EXPERTS_PER_GROUP)]
    m2 = functools.reduce(jnp.maximum, rest)
    i2 = jnp.full_like(g_idx, EXPERTS_PER_GROUP)
    for e in reversed(range(EXPERTS_PER_GROUP)):
        i2 = jnp.where(rest[e] == m2, e, i2)
    p2 = jnp.exp(m2 - m1)
    w1 = g_top / (1.0 + p2)
    w2 = g_top * p2 / (1.0 + p2)
    for g in range(N_GROUPS):
        route_ref[g:g + 1, :] = jnp.where(g_idx == g, 1.0, 0.0)
    for e in range(EXPERTS_PER_GROUP):
        route_ref[N_GROUPS + e:N_GROUPS + e + 1, :] = jnp.where(
            i1 == e, w1, jnp.where(i2 == e, w2, 0.0))


def _router(xt, w_group, b_group, w_router, b_router):
    t, d = xt.shape
    tm = ROUTE_TM
    rows = 32
    w_all = jnp.concatenate(
        [jnp.transpose(w_group), jnp.transpose(w_router, (0, 2, 1)).reshape(N_EXPERTS, d)], axis=0)
    w_all = jnp.pad(w_all, ((0, rows - w_all.shape[0]), (0, 0)))
    b_all = jnp.pad(jnp.concatenate([b_group, b_router.reshape(-1)]), (0, rows - N_GROUPS - N_EXPERTS))
    return pl.pallas_call(
        _router_kernel,
        grid=(t // tm,),
        in_specs=[pl.BlockSpec((tm, d), lambda i: (i, 0)),
                  pl.BlockSpec((rows, d), lambda i: (0, 0)),
                  pl.BlockSpec((rows, 1), lambda i: (0, 0))],
        out_specs=pl.BlockSpec((ROUTE_ROWS, tm), lambda i: (0, i)),
        out_shape=jax.ShapeDtypeStruct((ROUTE_ROWS, t), F32),
        compiler_params=pltpu.CompilerParams(
            dimension_semantics=("arbitrary",), vmem_limit_bytes=VMEM_LIMIT),
        name="router",
    )(xt, w_all, b_all.reshape(rows, 1))


def _plan_kernel(route_ref, ut_ref, plan_ref, dst_ref, tg_ref, cum_s, m_s, vals_s, cin_s):
    t = route_ref.shape[1]
    rt = ut_ref.shape[0]
    nk = t // rt
    nt, tr, _ = plan_ref.shape
    lane = lax.broadcasted_iota(jnp.int32, (1, rt), 1)

    def cum_body(k, carry):
        off = pl.multiple_of(k * rt, rt)
        route = route_ref[:, pl.ds(off, rt)]
        cum = jnp.dot(route.astype(BF16), ut_ref[...], preferred_element_type=F32) + carry
        comb = route[N_GROUPS:, :]
        b0 = comb.astype(BF16).astype(F32)
        b1 = (comb - b0).astype(BF16).astype(F32)
        b2 = (comb - b0 - b1).astype(BF16).astype(F32)
        tok = off + lane
        pieces = ([b0[e:e + 1] for e in range(EXPERTS_PER_GROUP)]
                  + [b1[e:e + 1] for e in range(EXPERTS_PER_GROUP)]
                  + [b2[e:e + 1] for e in range(EXPERTS_PER_GROUP)]
                  + [(tok >> 8).astype(F32), (tok & 255).astype(F32)])
        for r, piece in enumerate(pieces):
            vals_s[r:r + 1, pl.ds(off, rt)] = piece
        for r in range(len(pieces), PLAN_COLS):
            vals_s[r:r + 1, pl.ds(off, rt)] = jnp.zeros((1, rt), F32)
        for g in range(N_GROUPS):
            cum_s[g, :, pl.ds(off, rt)] = cum[g:g + 1, :]
            m_s[g, :, pl.ds(off, rt)] = route[g:g + 1, :]
            cin_s[k * N_GROUPS + g] = carry[g, 0].astype(jnp.int32)
        return cum[:, rt - 1:rt]

    total = lax.fori_loop(0, nk, cum_body, jnp.zeros((ROUTE_ROWS, 1), F32))
    tile_base = [jnp.int32(0)]
    for g in range(N_GROUPS):
        cnt = total[g, 0].astype(jnp.int32)
        cin_s[nk * N_GROUPS + g] = cnt
        tile_base.append(tile_base[-1] + (cnt + tr - 1) // tr)

    def dst_body(k, _):
        off = pl.multiple_of(k * rt, rt)
        pos = jnp.zeros((1, rt), F32)
        for g in range(N_GROUPS):
            pos = pos + m_s[g, :, pl.ds(off, rt)] * (
                cum_s[g, :, pl.ds(off, rt)] + (tile_base[g] * tr - 1).astype(F32))
        dst_ref[:, pl.ds(off, rt)] = pos.astype(jnp.int32)
        return 0

    lax.fori_loop(0, nk, dst_body, 0)

    row_iota = lax.broadcasted_iota(jnp.int32, (tr, rt), 0).astype(F32)

    def tile_body(i, k_prev):
        g = ((i >= tile_base[1]).astype(jnp.int32) + (i >= tile_base[2]).astype(jnp.int32)
             + (i >= tile_base[3]).astype(jnp.int32))
        base = jnp.where(g == 0, tile_base[0],
                         jnp.where(g == 1, tile_base[1], jnp.where(g == 2, tile_base[2], tile_base[3])))
        r0 = (i - base) * tr
        tg_ref[i] = g
        target = row_iota + (r0 + 1).astype(F32)

        k_first = lax.while_loop(
            lambda k: jnp.logical_and(k < nk, cin_s[jnp.minimum(k + 1, nk) * N_GROUPS + g] <= r0),
            lambda k: k + 1, jnp.where(i == base, 0, k_prev))

        def more(c):
            return jnp.logical_and(c[0] < nk, cin_s[c[0] * N_GROUPS + g] < r0 + tr)

        def pick(c):
            k, acc = c
            off = pl.multiple_of(k * rt, rt)
            onehot = jnp.where(cum_s[g, :, pl.ds(off, rt)] == target,
                               m_s[g, :, pl.ds(off, rt)], 0.0).astype(BF16)
            vals = vals_s[:, pl.ds(off, rt)].astype(BF16)
            return k + 1, acc + lax.dot_general(onehot, vals, (((1,), (1,)), ((), ())),
                                                preferred_element_type=F32)

        k_end, acc = lax.while_loop(more, pick, (k_first, jnp.zeros((tr, PLAN_COLS), F32)))
        plan_ref[i] = acc
        return jnp.maximum(k_end - 1, k_first)

    lax.fori_loop(0, nt, tile_body, jnp.int32(0))


def _plan(route):
    t = route.shape[1]
    rt = ROUTE_TM
    tr = MOE_TR
    nt = t // tr + N_GROUPS
    idx = jnp.arange(rt)
    ut = (idx[:, None] <= idx[None, :]).astype(BF16)
    return pl.pallas_call(
        _plan_kernel,
        in_specs=[pl.BlockSpec(memory_space=pltpu.VMEM), pl.BlockSpec(memory_space=pltpu.VMEM)],
        out_specs=[pl.BlockSpec(memory_space=pltpu.VMEM), pl.BlockSpec(memory_space=pltpu.VMEM),
                   pl.BlockSpec(memory_space=pltpu.SMEM)],
        out_shape=[jax.ShapeDtypeStruct((nt, tr, PLAN_COLS), F32),
                   jax.ShapeDtypeStruct((1, t), jnp.int32),
                   jax.ShapeDtypeStruct((nt,), jnp.int32)],
        scratch_shapes=[pltpu.VMEM((N_GROUPS, 1, t), F32), pltpu.VMEM((N_GROUPS, 1, t), F32),
                        pltpu.VMEM((PLAN_COLS, t), F32),
                        pltpu.SMEM(((t // rt + 1) * N_GROUPS,), jnp.int32)],
        compiler_params=pltpu.CompilerParams(vmem_limit_bytes=VMEM_LIMIT),
        name="moe_plan",
    )(route, ut)


def _start_row_gather(idx_ref, first, src_ref, buf, slot, sem):
    if not isinstance(slot, int):
        for s in range(buf.shape[0]):
            @pl.when(slot == s)
            def _():
                _start_row_gather(idx_ref, first, src_ref, buf, s, sem)
        return
    for r in range(buf.shape[1]):
        pltpu.make_async_copy(src_ref.at[pl.ds(idx_ref[first + r], 1), :],
                              buf.at[slot, pl.ds(r, 1), :], sem.at[slot]).start()


def _wait_row_gather(src_ref, buf, slot, sem):
    pltpu.make_async_copy(src_ref.at[pl.ds(0, buf.shape[1]), :], buf.at[slot], sem.at[slot]).wait()


def _ffn_kernel(tg_ref, src_ref, x_ref, plan_ref, w1_ref, w3_ref, w2_ref, ln_g_ref, ln_b_ref,
                o_ref, xbuf, sem):
    i = pl.program_id(0)
    last = pl.num_programs(0) - 1
    nslot, tr, _ = xbuf.shape
    slot = i % nslot

    @pl.when(i == 0)
    def _():
        for ahead in range(nslot - 1):
            _start_row_gather(src_ref, jnp.minimum(ahead, last) * tr, x_ref, xbuf, ahead, sem)

    _wait_row_gather(x_ref, xbuf, slot, sem)
    x = xbuf[slot]
    xb = x.astype(BF16)
    plan = plan_ref[0]
    y = jnp.zeros(x.shape, F32)
    for e in range(EXPERTS_PER_GROUP):
        n = EXPERTS_PER_GROUP
        c = plan[:, e:e + 1] + plan[:, n + e:n + e + 1] + plan[:, 2 * n + e:2 * n + e + 1]
        h1 = jnp.dot(xb, w1_ref[0, e], preferred_element_type=F32)
        h3 = jnp.dot(xb, w3_ref[0, e], preferred_element_type=F32)
        hid = (h1 * jax.nn.sigmoid(h1)) * h3 * c
        y = y + jnp.dot(hid.astype(BF16), w2_ref[0, e], preferred_element_type=F32)
    o_ref[...] = _layer_norm(DN_ALPHA * x + y, ln_g_ref[...], ln_b_ref[...])

    _start_row_gather(src_ref, jnp.minimum(i + nslot - 1, last) * tr, x_ref, xbuf,
                      (i + nslot - 1) % nslot, sem)

    @pl.when(i == last)
    def _():
        for ahead in range(1, nslot):
            _wait_row_gather(x_ref, xbuf, (i + ahead) % nslot, sem)


def _ffn(tile_group, src, xt, plan, w1, w3, w2, ln_g, ln_b):
    nt, tr, _ = plan.shape
    d, f = w1.shape[-2:]
    n = EXPERTS_PER_GROUP
    wspec = lambda shape: pl.BlockSpec((1,) + shape, lambda i, tg, src: (tg[i], 0, 0, 0))
    return pl.pallas_call(
        _ffn_kernel,
        grid_spec=pltpu.PrefetchScalarGridSpec(
            num_scalar_prefetch=2,
            grid=(nt,),
            in_specs=[pl.BlockSpec(memory_space=pl.ANY),
                      pl.BlockSpec((1, tr, PLAN_COLS), lambda i, tg, src: (i, 0, 0)),
                      wspec((n, d, f)), wspec((n, d, f)), wspec((n, f, d)),
                      pl.BlockSpec((1, d), lambda i, tg, src: (0, 0)),
                      pl.BlockSpec((1, d), lambda i, tg, src: (0, 0))],
            out_specs=pl.BlockSpec((tr, d), lambda i, tg, src: (i, 0)),
            scratch_shapes=[pltpu.VMEM((GATHER_SLOTS, tr, d), F32),
                            pltpu.SemaphoreType.DMA((GATHER_SLOTS,))]),
        out_shape=jax.ShapeDtypeStruct((nt * tr, d), F32),
        compiler_params=pltpu.CompilerParams(
            dimension_semantics=("arbitrary",), vmem_limit_bytes=VMEM_LIMIT),
        name="moe_ffn",
    )(tile_group, src, xt, plan, w1.astype(BF16), w3.astype(BF16), w2.astype(BF16),
      ln_g.reshape(1, -1), ln_b.reshape(1, -1))


def _unsort_kernel(dst_ref, y_ref, o_ref, buf, sem):
    i = pl.program_id(0)
    last = pl.num_programs(0) - 1
    nslot, tm, _ = buf.shape
    slot = i % nslot

    @pl.when(i == 0)
    def _():
        for ahead in range(nslot - 1):
            _start_row_gather(dst_ref, jnp.minimum(ahead, last) * tm, y_ref, buf, ahead, sem)

    _wait_row_gather(y_ref, buf, slot, sem)
    o_ref[...] = buf[slot]
    _start_row_gather(dst_ref, jnp.minimum(i + nslot - 1, last) * tm, y_ref, buf,
                      (i + nslot - 1) % nslot, sem)

    @pl.when(i == last)
    def _():
        for ahead in range(1, nslot):
            _wait_row_gather(y_ref, buf, (i + ahead) % nslot, sem)


def _unsort(dst, y, t):
    d = y.shape[1]
    tm = UNSORT_TM
    return pl.pallas_call(
        _unsort_kernel,
        grid_spec=pltpu.PrefetchScalarGridSpec(
            num_scalar_prefetch=1,
            grid=(t // tm,),
            in_specs=[pl.BlockSpec(memory_space=pl.ANY)],
            out_specs=pl.BlockSpec((tm, d), lambda i, dst: (i, 0)),
            scratch_shapes=[pltpu.VMEM((GATHER_SLOTS, tm, d), F32),
                            pltpu.SemaphoreType.DMA((GATHER_SLOTS,))]),
        out_shape=jax.ShapeDtypeStruct((t, d), F32),
        compiler_params=pltpu.CompilerParams(
            dimension_semantics=("arbitrary",), vmem_limit_bytes=VMEM_LIMIT),
        name="moe_unsort",
    )(dst, y)


def _moe_layer(x, w_group, b_group, w_router, b_router, w1, w3, w2, ln_g, ln_b):
    bsz, seq, d = x.shape
    xt = x.reshape(-1, d)
    route = _router(xt, w_group, b_group, w_router, b_router)
    plan, dst, tile_group = _plan(route)
    n = EXPERTS_PER_GROUP
    src = (plan[:, :, 3 * n] * 256.0 + plan[:, :, 3 * n + 1]).astype(jnp.int32).reshape(-1)
    y = _ffn(tile_group, src, xt, plan, w1, w3, w2, ln_g, ln_b)
    return _unsort(dst.reshape(-1), y, xt.shape[0]).reshape(bsz, seq, d)


def _qkv_kernel(x_ref, w_ref, o_ref):
    d = x_ref.shape[1]
    xb = x_ref[...].astype(BF16)
    for part in range(3):
        y = jnp.dot(xb, w_ref[:, part * d:(part + 1) * d], preferred_element_type=F32)
        if part == 0:
            y = y * (SB_HEAD_DIM ** -0.5 * LOG2_E)
        o_ref[:, part * d:(part + 1) * d] = y.astype(BF16)


def _qkv(xt, w_qkv):
    t, d = xt.shape
    tm = PROJ_TM
    return pl.pallas_call(
        _qkv_kernel,
        grid=(t // tm,),
        in_specs=[pl.BlockSpec((tm, d), lambda i: (i, 0)),
                  pl.BlockSpec((d, 3 * d), lambda i: (0, 0))],
        out_specs=pl.BlockSpec((tm, 3 * d), lambda i: (i, 0)),
        out_shape=jax.ShapeDtypeStruct((t, 3 * d), BF16),
        compiler_params=pltpu.CompilerParams(
            dimension_semantics=("arbitrary",), vmem_limit_bytes=VMEM_LIMIT),
        name="qkv_proj",
    )(xt, w_qkv.astype(BF16))


def _softplus2(z2):
    return jnp.maximum(z2, 0.0) + jnp.log2(1.0 + jnp.exp2(-jnp.abs(z2)))


def _attention_kernel(q_ref, k_ref, v_ref, o_ref):
    t = q_ref.shape[1]
    i = pl.program_id(2)
    q = q_ref[0]
    lane = lax.broadcasted_iota(jnp.int32, (t, LANES), 1)
    first_head = lane < SB_HEAD_DIM
    q_heads = (jnp.where(first_head, q, jnp.zeros_like(q)), jnp.where(first_head, jnp.zeros_like(q), q))
    row = lax.broadcasted_iota(jnp.int32, (t, t), 0)
    col = lax.broadcasted_iota(jnp.int32, (t, t), 1)
    tri = jnp.where(row >= col, 1.0, 0.0).astype(BF16)
    strict = col < row

    def block(kb, acc, r0, r1, diagonal):
        start = pl.multiple_of(kb * t, t)
        k = k_ref[0, pl.ds(start, t), :]
        v = v_ref[0, pl.ds(start, t), :]
        rs = (r0, r1)
        tc = t // ATT_SPLIT
        rows = [slice(j * tc, (j + 1) * tc) for j in range(ATT_SPLIT)]
        chains = [(h, j) for j in range(ATT_SPLIT) for h in range(2)]
        zs = {(h, j): lax.dot_general(q_heads[h][rows[j]], k, (((1,), (1,)), ((), ())),
                                      preferred_element_type=F32) for h, j in chains}
        sps, cs, atts, outs = {}, {}, {}, {}

        def sp_of(c):
            sp = _softplus2(zs[c])
            if diagonal:
                sp = jnp.where(strict[rows[c[1]]], sp, 0.0)
            return sp.astype(BF16)

        def att_of(c):
            att = jnp.exp2(zs[c] - cs[c] - rs[c[0]][rows[c[1]]])
            if diagonal:
                att = jnp.where(strict[rows[c[1]]], att, 0.0)
            return att.astype(BF16)

        sps[chains[0]] = sp_of(chains[0])
        for n, c in enumerate(chains):
            cs[c] = jnp.dot(sps[c], tri, preferred_element_type=F32)
            if n + 1 < len(chains):
                sps[chains[n + 1]] = sp_of(chains[n + 1])
        r0n, r1n = [rs[h] + jnp.concatenate([cs[(h, j)][:, 0:1] for j in range(ATT_SPLIT)], axis=0)
                    for h in range(2)]
        r_min = jnp.min(jnp.minimum(r0n, r1n))
        atts[chains[0]] = att_of(chains[0])
        for n, c in enumerate(chains):
            outs[c] = jnp.dot(atts[c], v, preferred_element_type=F32)
            if n + 1 < len(chains):
                atts[chains[n + 1]] = att_of(chains[n + 1])
        pv = [jnp.concatenate([outs[(h, j)] for j in range(ATT_SPLIT)], axis=0) for h in range(2)]
        acc = acc + jnp.where(first_head, pv[0], pv[1])
        return acc, r0n, r1n, r_min

    zero_r = jnp.zeros((t, 1), F32)
    acc, r0, r1, r_min = block(i, jnp.zeros((t, LANES), F32), zero_r, zero_r, True)

    def cond(c):
        return jnp.logical_and(c[0] >= 0, c[4] < ATT_DEAD)

    def body(c):
        kb, acc, r0, r1, _ = c
        acc, r0, r1, r_min = block(kb, acc, r0, r1, False)
        return kb - 1, acc, r0, r1, r_min

    carry = lax.while_loop(cond, body, (i - 1, acc, r0, r1, r_min))
    o_ref[0] = carry[1].astype(o_ref.dtype)


def _attention(qkv, bsz, seq):
    d = qkv.shape[-1] // 3
    qkv = qkv.reshape(bsz, seq, 3 * d)
    t = ATT_T
    npair = d // LANES
    return pl.pallas_call(
        _attention_kernel,
        grid=(bsz, npair, seq // t),
        in_specs=[pl.BlockSpec((1, t, LANES), lambda b, p, i: (b, i, p)),
                  pl.BlockSpec((1, seq, LANES), lambda b, p, i: (b, 0, npair + p)),
                  pl.BlockSpec((1, seq, LANES), lambda b, p, i: (b, 0, 2 * npair + p))],
        out_specs=pl.BlockSpec((1, t, LANES), lambda b, p, i: (b, i, p)),
        out_shape=jax.ShapeDtypeStruct((bsz, seq, d), BF16),
        compiler_params=pltpu.CompilerParams(
            dimension_semantics=("arbitrary", "arbitrary", "arbitrary"), vmem_limit_bytes=VMEM_LIMIT),
        name="sb_attention",
    )(qkv, qkv, qkv)


def _proj_ln_kernel(y_ref, x_ref, w_ref, ln_g_ref, ln_b_ref, o_ref):
    mix = jnp.dot(y_ref[...], w_ref[...], preferred_element_type=F32)
    o_ref[...] = _layer_norm(DN_ALPHA * x_ref[...] + mix, ln_g_ref[...], ln_b_ref[...])


def _proj_ln(y, xt, w, ln_g, ln_b):
    t, d = xt.shape
    tm = PROJ_TM
    return pl.pallas_call(
        _proj_ln_kernel,
        grid=(t // tm,),
        in_specs=[pl.BlockSpec((tm, d), lambda i: (i, 0)),
                  pl.BlockSpec((tm, d), lambda i: (i, 0)),
                  pl.BlockSpec((d, d), lambda i: (0, 0)),
                  pl.BlockSpec((1, d), lambda i: (0, 0)),
                  pl.BlockSpec((1, d), lambda i: (0, 0))],
        out_specs=pl.BlockSpec((tm, d), lambda i: (i, 0)),
        out_shape=jax.ShapeDtypeStruct((t, d), F32),
        compiler_params=pltpu.CompilerParams(
            dimension_semantics=("arbitrary",), vmem_limit_bytes=VMEM_LIMIT),
        name="out_proj_ln",
    )(y, xt, w.astype(BF16), ln_g.reshape(1, -1), ln_b.reshape(1, -1))


def _sb_mixer(x, w_qkv, w_out, ln_g, ln_b):
    bsz, seq, d = x.shape
    xt = x.reshape(-1, d)
    o = _attention(_qkv(xt, w_qkv), bsz, seq)
    return _proj_ln(o.reshape(-1, d), xt, w_out, ln_g, ln_b).reshape(bsz, seq, d)


def kernel(x, even_w_in, even_conv_w, even_sgu_ln_g, even_sgu_ln_b, even_sgu_w_s, even_sgu_b_s, even_w_out, odd_w_qkv, odd_w_out, mix_ln_g, mix_ln_b, moe_w_group, moe_b_group, moe_w_router, moe_b_router, moe_w1, moe_w3, moe_w2, ffn_ln_g, ffn_ln_b):
    for layer in range(DEPTH):
        i = layer // 2
        if layer % 2 == 0:
            x = _mixer0(x, even_w_in[i], even_conv_w[i], even_sgu_ln_g[i], even_sgu_ln_b[i],
                        even_sgu_w_s[i], even_sgu_b_s[i], even_w_out[i], mix_ln_g[layer], mix_ln_b[layer])
        else:
            x = _sb_mixer(x, odd_w_qkv[i], odd_w_out[i], mix_ln_g[layer], mix_ln_b[layer])
        x = _moe_layer(x, moe_w_group[layer], moe_b_group[layer], moe_w_router[layer],
                       moe_b_router[layer], moe_w1[layer], moe_w3[layer], moe_w2[layer],
                       ffn_ln_g[layer], ffn_ln_b[layer])
    return x
```

```python
import functools

import jax
import jax.numpy as jnp
from jax import lax
from jax.experimental import pallas as pl
from jax.experimental.pallas import tpu as pltpu

F32 = jnp.float32
BF16 = jnp.bfloat16

DEPTH = 2
DN_ALPHA = (2 * DEPTH) ** 0.25
LN_EPS = 1e-5

CONV_WIDTH = 512
CONV_K = 3
SGU_WIDTH = 512
SGU_HEADS = 8
SGU_HEAD_DIM = 64
SGU_BLOCK = 128
CHUNK = 64
SB_HEADS = 16
SB_HEAD_DIM = 64
N_GROUPS = 4
EXPERTS_PER_GROUP = 4
N_EXPERTS = N_GROUPS * EXPERTS_PER_GROUP

LANES = 128
VMEM_LIMIT = 56 * 1024 * 1024

MIX_TM = 512
PROJ_TM = 512
ROUTE_TM = 1024
ROUTE_ROWS = 8
PLAN_COLS = 16
MOE_TR = 256
GATHER_SLOTS = 3
UNSORT_TM = 256
ATT_T = 256
LOG2_E = 1.4426950408889634
ATT_DEAD = 150.0


def _layer_norm(r, g, b):
    mu = jnp.mean(r, axis=-1, keepdims=True)
    c = r - mu
    var = jnp.mean(c * c, axis=-1, keepdims=True)
    return c * lax.rsqrt(var + LN_EPS) * g + b


def _gelu_tanh(x):
    return 0.5 * x * (1.0 + jnp.tanh(0.7978845608028654 * (x + 0.044715 * (x * x * x))))


def _mixer0_kernel(x_ref, w_in_ref, conv_w_ref, sgu_g_ref, sgu_b_ref, ws_ref, bs_ref, w_out_ref,
                   ln_g_ref, ln_b_ref, o_ref, tail_ref, y_ref):
    tm = x_ref.shape[1]

    @pl.when(pl.program_id(1) == 0)
    def _():
        tail_ref[...] = jnp.zeros_like(tail_ref)

    x = x_ref[0]
    xb = x.astype(BF16)
    cw = CONV_WIDTH
    bch = jnp.dot(xb, w_in_ref[:, :3 * cw], preferred_element_type=F32)
    b_gate, c_gate, h = bch[:, :cw], bch[:, cw:2 * cw], bch[:, 2 * cw:]
    u = c_gate * h
    cat = jnp.concatenate([tail_ref[...], u], axis=0)
    tail_ref[...] = u[tm - 8:, :]
    u1 = cat[7:7 + tm, :]
    u2 = cat[6:6 + tm, :]
    conv = conv_w_ref[0:1, :] * u2 + conv_w_ref[1:2, :] * u1 + conv_w_ref[2:3, :] * u
    y_ref[:, :cw] = (b_gate * conv).astype(BF16)

    z = _gelu_tanh(jnp.dot(xb, w_in_ref[:, 3 * cw:], preferred_element_type=F32))
    z_u, z_v = z[:, :SGU_WIDTH], z[:, SGU_WIDTH:]
    v = _layer_norm(z_v, sgu_g_ref[...], sgu_b_ref[...])
    lane = lax.broadcasted_iota(jnp.int32, (tm, LANES), 1)
    first_head = lane < SGU_HEAD_DIM
    nblk = tm // SGU_BLOCK
    for j in range(SGU_HEADS // 2):
        vp = v[:, j * LANES:(j + 1) * LANES]
        v_a = jnp.where(first_head, vp, 0.0).astype(BF16)
        v_b = jnp.where(first_head, 0.0, vp).astype(BF16)
        rhs = jnp.concatenate(
            [jnp.concatenate([v_a[n * SGU_BLOCK:(n + 1) * SGU_BLOCK],
                              v_b[n * SGU_BLOCK:(n + 1) * SGU_BLOCK]], axis=0)
             for n in range(nblk)], axis=1)
        mixed = jnp.dot(ws_ref[j], rhs, preferred_element_type=F32)
        bias = bs_ref[:, j * LANES:(j + 1) * LANES]
        for n in range(nblk):
            rows = slice(n * SGU_BLOCK, (n + 1) * SGU_BLOCK)
            gate = z_u[rows, j * LANES:(j + 1) * LANES]
            y_ref[rows, cw + j * LANES:cw + (j + 1) * LANES] = (
                gate * (mixed[:, n * LANES:(n + 1) * LANES] + bias)).astype(BF16)

    mix = jnp.dot(y_ref[...], w_out_ref[...], preferred_element_type=F32)
    o_ref[0] = _layer_norm(DN_ALPHA * x + mix, ln_g_ref[...], ln_b_ref[...])


def _mixer0(x, w_in, conv_w, sgu_g, sgu_b, w_s, b_s, w_out, ln_g, ln_b):
    bsz, seq, d = x.shape
    tm = MIX_TM
    pos = jnp.arange(SGU_BLOCK)
    chunk_causal = (pos[None, :] // CHUNK) <= (pos[:, None] // CHUNK)
    w_masked = jnp.where(chunk_causal[None], w_s, 0.0).astype(BF16)
    ws_pairs = jnp.concatenate([w_masked[0::2], w_masked[1::2]], axis=2)
    bias = jnp.repeat(jnp.transpose(b_s), SGU_HEAD_DIM, axis=1)
    full = lambda shape: pl.BlockSpec(shape, lambda b, s: (0,) * len(shape))
    return pl.pallas_call(
        _mixer0_kernel,
        grid=(bsz, seq // tm),
        in_specs=[
            pl.BlockSpec((1, tm, d), lambda b, s: (b, s, 0)),
            full(w_in.shape), full(conv_w.shape), full((1, SGU_WIDTH)), full((1, SGU_WIDTH)),
            full(ws_pairs.shape), full(bias.shape), full(w_out.shape),
            full((1, d)), full((1, d)),
        ],
        out_specs=pl.BlockSpec((1, tm, d), lambda b, s: (b, s, 0)),
        out_shape=jax.ShapeDtypeStruct(x.shape, F32),
        scratch_shapes=[pltpu.VMEM((8, CONV_WIDTH), F32), pltpu.VMEM((tm, d), BF16)],
        compiler_params=pltpu.CompilerParams(
            dimension_semantics=("arbitrary", "arbitrary"), vmem_limit_bytes=VMEM_LIMIT),
        name="mixer0",
    )(x, w_in.astype(BF16), conv_w, sgu_g.reshape(1, -1), sgu_b.reshape(1, -1), ws_pairs, bias,
      w_out.astype(BF16), ln_g.reshape(1, -1), ln_b.reshape(1, -1))


def _router_kernel(x_ref, w_ref, b_ref, route_ref):
    x = x_ref[...]
    xh = x.astype(BF16)
    xl = (x - xh.astype(F32)).astype(BF16)
    w = w_ref[...]
    wh = w.astype(BF16)
    wl = (w - wh.astype(F32)).astype(BF16)
    logits_t = (jnp.dot(xh, wh, preferred_element_type=F32)
                + jnp.dot(xl, wh, preferred_element_type=F32)
                + jnp.dot(xh, wl, preferred_element_type=F32))
    logits = jnp.transpose(logits_t)[:b_ref.shape[0], :] + b_ref[...]
    gl = [logits[g:g + 1, :] for g in range(N_GROUPS)]
    g_max = functools.reduce(jnp.maximum, gl)
    g_top = 1.0 / functools.reduce(lambda a, b: a + b, [jnp.exp(l - g_max) for l in gl])
    g_idx = jnp.full_like(g_max, N_GROUPS).astype(jnp.int32)
    for g in reversed(range(N_GROUPS)):
        g_idx = jnp.where(gl[g] == g_max, g, g_idx)
    el = []
    for e in range(EXPERTS_PER_GROUP):
        acc = jnp.zeros_like(g_max)
        for g in range(N_GROUPS):
            r = N_GROUPS + g * EXPERTS_PER_GROUP + e
            acc = jnp.where(g_idx == g, logits[r:r + 1, :], acc)
        el.append(acc)
    m1 = functools.reduce(jnp.maximum, el)
    i1 = jnp.full_like(g_idx, EXPERTS_PER_GROUP)
    for e in reversed(range(EXPERTS_PER_GROUP)):
        i1 = jnp.where(el[e] == m1, e, i1)
    rest = [jnp.where(i1 == e, -jnp.inf, el[e]) for e in range(EXPERTS_PER_GROUP)]
    m2 = functools.reduce(jnp.maximum, rest)
    i2 = jnp.full_like(g_idx, EXPERTS_PER_GROUP)
    for e in reversed(range(EXPERTS_PER_GROUP)):
        i2 = jnp.where(rest[e] == m2, e, i2)
    p2 = jnp.exp(m2 - m1)
    w1 = g_top / (1.0 + p2)
    w2 = g_top * p2 / (1.0 + p2)
    for g in range(N_GROUPS):
        route_ref[g:g + 1, :] = jnp.where(g_idx == g, 1.0, 0.0)
    for e in range(EXPERTS_PER_GROUP):
        route_ref[N_GROUPS + e:N_GROUPS + e + 1, :] = jnp.where(
            i1 == e, w1, jnp.where(i2 == e, w2, 0.0))


def _router(xt, w_group, b_group, w_router, b_router):
    t, d = xt.shape
    tm = ROUTE_TM
    rows = 32
    w_all = jnp.concatenate(
        [jnp.transpose(w_group), jnp.transpose(w_router, (0, 2, 1)).reshape(N_EXPERTS, d)], axis=0)
    w_t = jnp.transpose(jnp.pad(w_all, ((0, LANES - w_all.shape[0]), (0, 0))))
    b_all = jnp.pad(jnp.concatenate([b_group, b_router.reshape(-1)]), (0, rows - N_GROUPS - N_EXPERTS))
    return pl.pallas_call(
        _router_kernel,
        grid=(t // tm,),
        in_specs=[pl.BlockSpec((tm, d), lambda i: (i, 0)),
                  pl.BlockSpec((d, LANES), lambda i: (0, 0)),
                  pl.BlockSpec((rows, 1), lambda i: (0, 0))],
        out_specs=pl.BlockSpec((ROUTE_ROWS, tm), lambda i: (0, i)),
        out_shape=jax.ShapeDtypeStruct((ROUTE_ROWS, t), F32),
        compiler_params=pltpu.CompilerParams(
            dimension_semantics=("arbitrary",), vmem_limit_bytes=VMEM_LIMIT),
        name="router",
    )(xt, w_t, b_all.reshape(rows, 1))


def _plan_kernel(route_ref, ut_ref, plan_ref, src_ref, dst_ref, tg_ref, cum_s, m_s, vals_s, cin_s):
    t = route_ref.shape[1]
    rt = ut_ref.shape[0]
    nk = t // rt
    nt, tr, _ = plan_ref.shape
    lane = lax.broadcasted_iota(jnp.int32, (1, rt), 1)

    def cum_body(k, carry):
        off = pl.multiple_of(k * rt, rt)
        route = route_ref[:, pl.ds(off, rt)]
        cum = jnp.dot(route.astype(BF16), ut_ref[...], preferred_element_type=F32) + carry
        comb = route[N_GROUPS:, :]
        b0 = comb.astype(BF16).astype(F32)
        b1 = (comb - b0).astype(BF16).astype(F32)
        b2 = (comb - b0 - b1).astype(BF16).astype(F32)
        tok = off + lane
        pieces = ([b0[e:e + 1] for e in range(EXPERTS_PER_GROUP)]
                  + [b1[e:e + 1] for e in range(EXPERTS_PER_GROUP)]
                  + [b2[e:e + 1] for e in range(EXPERTS_PER_GROUP)]
                  + [(tok >> 8).astype(F32), (tok & 255).astype(F32)])
        for r, piece in enumerate(pieces):
            vals_s[r:r + 1, pl.ds(off, rt)] = piece
        for r in range(len(pieces), PLAN_COLS):
            vals_s[r:r + 1, pl.ds(off, rt)] = jnp.zeros((1, rt), F32)
        for g in range(N_GROUPS):
            cum_s[g, :, pl.ds(off, rt)] = cum[g:g + 1, :]
            m_s[g, :, pl.ds(off, rt)] = route[g:g + 1, :]
            cin_s[k * N_GROUPS + g] = carry[g, 0].astype(jnp.int32)
        return cum[:, rt - 1:rt]

    total = lax.fori_loop(0, nk, cum_body, jnp.zeros((ROUTE_ROWS, 1), F32))
    tile_base = [jnp.int32(0)]
    for g in range(N_GROUPS):
        cnt = total[g, 0].astype(jnp.int32)
        cin_s[nk * N_GROUPS + g] = cnt
        tile_base.append(tile_base[-1] + (cnt + tr - 1) // tr)

    def dst_body(k, _):
        off = pl.multiple_of(k * rt, rt)
        pos = jnp.zeros((1, rt), F32)
        for g in range(N_GROUPS):
            pos = pos + m_s[g, :, pl.ds(off, rt)] * (
                cum_s[g, :, pl.ds(off, rt)] + (tile_base[g] * tr - 1).astype(F32))
        dst_ref[:, pl.ds(off, rt)] = pos.astype(jnp.int32)
        return 0

    lax.fori_loop(0, nk, dst_body, 0)

    row_iota = lax.broadcasted_iota(jnp.int32, (tr, rt), 0).astype(F32)

    def tile_body(i, k_prev):
        g = ((i >= tile_base[1]).astype(jnp.int32) + (i >= tile_base[2]).astype(jnp.int32)
             + (i >= tile_base[3]).astype(jnp.int32))
        base = jnp.where(g == 0, tile_base[0],
                         jnp.where(g == 1, tile_base[1], jnp.where(g == 2, tile_base[2], tile_base[3])))
        r0 = (i - base) * tr
        tg_ref[i] = g
        target = row_iota + (r0 + 1).astype(F32)

        k_first = lax.while_loop(
            lambda k: jnp.logical_and(k < nk, cin_s[jnp.minimum(k + 1, nk) * N_GROUPS + g] <= r0),
            lambda k: k + 1, jnp.where(i == base, 0, k_prev))

        def more(c):
            return jnp.logical_and(c[0] < nk, cin_s[c[0] * N_GROUPS + g] < r0 + tr)

        def pick(c):
            k, acc = c
            off = pl.multiple_of(k * rt, rt)
            onehot = jnp.where(cum_s[g, :, pl.ds(off, rt)] == target,
                               m_s[g, :, pl.ds(off, rt)], 0.0).astype(BF16)
            vals = vals_s[:, pl.ds(off, rt)].astype(BF16)
            return k + 1, acc + lax.dot_general(onehot, vals, (((1,), (1,)), ((), ())),
                                                preferred_element_type=F32)

        k_end, acc = lax.while_loop(more, pick, (k_first, jnp.zeros((tr, PLAN_COLS), F32)))
        plan_ref[i] = acc
        digits = jnp.transpose(jnp.concatenate(
            [acc, jnp.zeros((tr, LANES - PLAN_COLS), F32)], axis=1))
        n = 3 * EXPERTS_PER_GROUP
        src_ref[i] = (digits[n:n + 1, :] * 256.0 + digits[n + 1:n + 2, :]).astype(jnp.int32)
        return jnp.maximum(k_end - 1, k_first)

    lax.fori_loop(0, nt, tile_body, jnp.int32(0))


def _plan(route):
    t = route.shape[1]
    rt = ROUTE_TM
    tr = MOE_TR
    nt = t // tr + N_GROUPS
    idx = jnp.arange(rt)
    ut = (idx[:, None] <= idx[None, :]).astype(BF16)
    return pl.pallas_call(
        _plan_kernel,
        in_specs=[pl.BlockSpec(memory_space=pltpu.VMEM), pl.BlockSpec(memory_space=pltpu.VMEM)],
        out_specs=[pl.BlockSpec(memory_space=pltpu.VMEM), pl.BlockSpec(memory_space=pltpu.VMEM),
                   pl.BlockSpec(memory_space=pltpu.VMEM), pl.BlockSpec(memory_space=pltpu.SMEM)],
        out_shape=[jax.ShapeDtypeStruct((nt, tr, PLAN_COLS), F32),
                   jax.ShapeDtypeStruct((nt, 1, tr), jnp.int32),
                   jax.ShapeDtypeStruct((1, t), jnp.int32),
                   jax.ShapeDtypeStruct((nt,), jnp.int32)],
        scratch_shapes=[pltpu.VMEM((N_GROUPS, 1, t), F32), pltpu.VMEM((N_GROUPS, 1, t), F32),
                        pltpu.VMEM((PLAN_COLS, t), F32),
                        pltpu.SMEM(((t // rt + 1) * N_GROUPS,), jnp.int32)],
        compiler_params=pltpu.CompilerParams(vmem_limit_bytes=VMEM_LIMIT),
        name="moe_plan",
    )(route, ut)


def _start_row_gather(idx_ref, first, src_ref, buf, slot, sem):
    if not isinstance(slot, int):
        for s in range(buf.shape[0]):
            @pl.when(slot == s)
            def _():
                _start_row_gather(idx_ref, first, src_ref, buf, s, sem)
        return
    for r in range(buf.shape[1]):
        pltpu.make_async_copy(src_ref.at[pl.ds(idx_ref[first + r], 1), :],
                              buf.at[slot, pl.ds(r, 1), :], sem.at[slot]).start()


def _wait_row_gather(src_ref, buf, slot, sem):
    pltpu.make_async_copy(src_ref.at[pl.ds(0, buf.shape[1]), :], buf.at[slot], sem.at[slot]).wait()


def _ffn_kernel(tg_ref, src_ref, x_ref, plan_ref, w1_ref, w3_ref, w2_ref, ln_g_ref, ln_b_ref,
                o_ref, xbuf, sem):
    i = pl.program_id(0)
    last = pl.num_programs(0) - 1
    nslot, tr, _ = xbuf.shape
    slot = i % nslot

    @pl.when(i == 0)
    def _():
        for ahead in range(nslot - 1):
            _start_row_gather(src_ref, jnp.minimum(ahead, last) * tr, x_ref, xbuf, ahead, sem)

    _wait_row_gather(x_ref, xbuf, slot, sem)
    x = xbuf[slot]
    xb = x.astype(BF16)
    plan = plan_ref[0]
    y = jnp.zeros(x.shape, F32)
    for e in range(EXPERTS_PER_GROUP):
        n = EXPERTS_PER_GROUP
        c = plan[:, e:e + 1] + plan[:, n + e:n + e + 1] + plan[:, 2 * n + e:2 * n + e + 1]
        h1 = jnp.dot(xb, w1_ref[0, e], preferred_element_type=F32)
        h3 = jnp.dot(xb, w3_ref[0, e], preferred_element_type=F32)
        hid = (h1 * jax.nn.sigmoid(h1)) * h3 * c
        y = y + jnp.dot(hid.astype(BF16), w2_ref[0, e], preferred_element_type=F32)
    o_ref[...] = _layer_norm(DN_ALPHA * x + y, ln_g_ref[...], ln_b_ref[...])

    _start_row_gather(src_ref, jnp.minimum(i + nslot - 1, last) * tr, x_ref, xbuf,
                      (i + nslot - 1) % nslot, sem)

    @pl.when(i == last)
    def _():
        for ahead in range(1, nslot):
            _wait_row_gather(x_ref, xbuf, (i + ahead) % nslot, sem)


def _ffn(tile_group, src, xt, plan, w1, w3, w2, ln_g, ln_b):
    nt, tr, _ = plan.shape
    d, f = w1.shape[-2:]
    n = EXPERTS_PER_GROUP
    wspec = lambda shape: pl.BlockSpec((1,) + shape, lambda i, tg, src: (tg[i], 0, 0, 0))
    return pl.pallas_call(
        _ffn_kernel,
        grid_spec=pltpu.PrefetchScalarGridSpec(
            num_scalar_prefetch=2,
            grid=(nt,),
            in_specs=[pl.BlockSpec(memory_space=pl.ANY),
                      pl.BlockSpec((1, tr, PLAN_COLS), lambda i, tg, src: (i, 0, 0)),
                      wspec((n, d, f)), wspec((n, d, f)), wspec((n, f, d)),
                      pl.BlockSpec((1, d), lambda i, tg, src: (0, 0)),
                      pl.BlockSpec((1, d), lambda i, tg, src: (0, 0))],
            out_specs=pl.BlockSpec((tr, d), lambda i, tg, src: (i, 0)),
            scratch_shapes=[pltpu.VMEM((GATHER_SLOTS, tr, d), F32),
                            pltpu.SemaphoreType.DMA((GATHER_SLOTS,))]),
        out_shape=jax.ShapeDtypeStruct((nt * tr, d), F32),
        compiler_params=pltpu.CompilerParams(
            dimension_semantics=("arbitrary",), vmem_limit_bytes=VMEM_LIMIT),
        name="moe_ffn",
    )(tile_group, src, xt, plan, w1.astype(BF16), w3.astype(BF16), w2.astype(BF16),
      ln_g.reshape(1, -1), ln_b.reshape(1, -1))


def _unsort_kernel(dst_ref, y_ref, o_ref, buf, sem):
    i = pl.program_id(0)
    last = pl.num_programs(0) - 1
    nslot, tm, _ = buf.shape
    slot = i % nslot

    @pl.when(i == 0)
    def _():
        for ahead in range(nslot - 1):
            _start_row_gather(dst_ref, jnp.minimum(ahead, last) * tm, y_ref, buf, ahead, sem)

    _wait_row_gather(y_ref, buf, slot, sem)
    o_ref[...] = buf[slot]
    _start_row_gather(dst_ref, jnp.minimum(i + nslot - 1, last) * tm, y_ref, buf,
                      (i + nslot - 1) % nslot, sem)

    @pl.when(i == last)
    def _():
        for ahead in range(1, nslot):
            _wait_row_gather(y_ref, buf, (i + ahead) % nslot, sem)


def _unsort(dst, y, t):
    d = y.shape[1]
    tm = UNSORT_TM
    return pl.pallas_call(
        _unsort_kernel,
        grid_spec=pltpu.PrefetchScalarGridSpec(
            num_scalar_prefetch=1,
            grid=(t // tm,),
            in_specs=[pl.BlockSpec(memory_space=pl.ANY)],
            out_specs=pl.BlockSpec((tm, d), lambda i, dst: (i, 0)),
            scratch_shapes=[pltpu.VMEM((GATHER_SLOTS, tm, d), F32),
                            pltpu.SemaphoreType.DMA((GATHER_SLOTS,))]),
        out_shape=jax.ShapeDtypeStruct((t, d), F32),
        compiler_params=pltpu.CompilerParams(
            dimension_semantics=("arbitrary",), vmem_limit_bytes=VMEM_LIMIT),
        name="moe_unsort",
    )(dst, y)


def _moe_layer(x, w_group, b_group, w_router, b_router, w1, w3, w2, ln_g, ln_b):
    bsz, seq, d = x.shape
    xt = x.reshape(-1, d)
    route = _router(xt, w_group, b_group, w_router, b_router)
    plan, src, dst, tile_group = _plan(route)
    y = _ffn(tile_group, src.reshape(-1), xt, plan, w1, w3, w2, ln_g, ln_b)
    return _unsort(dst.reshape(-1), y, xt.shape[0]).reshape(bsz, seq, d)


def _qkv_kernel(x_ref, w_ref, o_ref):
    d = x_ref.shape[1]
    xb = x_ref[...].astype(BF16)
    for part in range(3):
        y = jnp.dot(xb, w_ref[:, part * d:(part + 1) * d], preferred_element_type=F32)
        if part == 0:
            y = y * (SB_HEAD_DIM ** -0.5 * LOG2_E)
        o_ref[:, part * d:(part + 1) * d] = y.astype(BF16)


def _qkv(xt, w_qkv):
    t, d = xt.shape
    tm = PROJ_TM
    return pl.pallas_call(
        _qkv_kernel,
        grid=(t // tm,),
        in_specs=[pl.BlockSpec((tm, d), lambda i: (i, 0)),
                  pl.BlockSpec((d, 3 * d), lambda i: (0, 0))],
        out_specs=pl.BlockSpec((tm, 3 * d), lambda i: (i, 0)),
        out_shape=jax.ShapeDtypeStruct((t, 3 * d), BF16),
        compiler_params=pltpu.CompilerParams(
            dimension_semantics=("arbitrary",), vmem_limit_bytes=VMEM_LIMIT),
        name="qkv_proj",
    )(xt, w_qkv.astype(BF16))


def _softplus2(z2):
    return jnp.maximum(z2, 0.0) + jnp.log2(1.0 + jnp.exp2(-jnp.abs(z2)))


def _attention_kernel(q_ref, k_ref, v_ref, o_ref):
    t = ATT_T
    nq = q_ref.shape[1] // t
    nt_dims = (((1,), (1,)), ((), ()))
    lane = lax.broadcasted_iota(jnp.int32, (t, LANES), 1)
    first_head = lane < SB_HEAD_DIM
    row = lax.broadcasted_iota(jnp.int32, (t, t), 0)
    col = lax.broadcasted_iota(jnp.int32, (t, t), 1)
    tri = jnp.where(row >= col, 1.0, 0.0).astype(BF16)
    strict = col < row

    def tile(i, _):
        q_start = pl.multiple_of(i * t, t)
        p_start = pl.multiple_of(jnp.maximum(i - 1, 0) * t, t)
        has_prev = i > 0
        q = q_ref[0, pl.ds(q_start, t), :]
        q_heads = (jnp.where(first_head, q, jnp.zeros_like(q)),
                   jnp.where(first_head, jnp.zeros_like(q), q))
        keys = {"d": k_ref[0, pl.ds(q_start, t), :], "p": k_ref[0, pl.ds(p_start, t), :]}
        vals = {"d": v_ref[0, pl.ds(q_start, t), :], "p": v_ref[0, pl.ds(p_start, t), :]}

        chains = [(h, b) for b in ("d", "p") for h in range(2)]
        zs = {c: lax.dot_general(q_heads[c[0]], keys[c[1]], nt_dims, preferred_element_type=F32)
              for c in chains}
        sps, cs, atts, outs = {}, {}, {}, {}

        def keep(c, x):
            return jnp.where(strict, x, 0.0) if c[1] == "d" else jnp.where(has_prev, x, 0.0)

        def sp_of(c):
            return keep(c, _softplus2(zs[c])).astype(BF16)

        def att_of(c):
            e = zs[c] - cs[c]
            if c[1] == "p":
                e = e - cs[(c[0], "d")][:, 0:1]
            return keep(c, jnp.exp2(e)).astype(BF16)

        sps[chains[0]] = sp_of(chains[0])
        for n, c in enumerate(chains):
            cs[c] = jnp.dot(sps[c], tri, preferred_element_type=F32)
            if n + 1 < len(chains):
                sps[chains[n + 1]] = sp_of(chains[n + 1])
        r0, r1 = [cs[(h, "d")][:, 0:1] + cs[(h, "p")][:, 0:1] for h in range(2)]
        r_min = jnp.min(jnp.minimum(r0, r1))
        atts[chains[0]] = att_of(chains[0])
        for n, c in enumerate(chains):
            outs[c] = jnp.dot(atts[c], vals[c[1]], preferred_element_type=F32)
            if n + 1 < len(chains):
                atts[chains[n + 1]] = att_of(chains[n + 1])
        acc = jnp.where(first_head, outs[(0, "d")] + outs[(0, "p")], outs[(1, "d")] + outs[(1, "p")])

        def block(kb, acc, r0, r1):
            start = pl.multiple_of(kb * t, t)
            k = k_ref[0, pl.ds(start, t), :]
            v = v_ref[0, pl.ds(start, t), :]
            z = [lax.dot_general(qh, k, nt_dims, preferred_element_type=F32) for qh in q_heads]
            sp0 = _softplus2(z[0]).astype(BF16)
            c0 = jnp.dot(sp0, tri, preferred_element_type=F32)
            sp1 = _softplus2(z[1]).astype(BF16)
            c1 = jnp.dot(sp1, tri, preferred_element_type=F32)
            att0 = jnp.exp2(z[0] - c0 - r0).astype(BF16)
            o0 = jnp.dot(att0, v, preferred_element_type=F32)
            att1 = jnp.exp2(z[1] - c1 - r1).astype(BF16)
            o1 = jnp.dot(att1, v, preferred_element_type=F32)
            r0n = r0 + c0[:, 0:1]
            r1n = r1 + c1[:, 0:1]
            return (acc + jnp.where(first_head, o0, o1), r0n, r1n,
                    jnp.min(jnp.minimum(r0n, r1n)))

        def cond(c):
            return jnp.logical_and(c[0] >= 0, c[4] < ATT_DEAD)

        def body(c):
            kb, acc, r0, r1, _ = c
            acc, r0, r1, r_min = block(kb, acc, r0, r1)
            return kb - 1, acc, r0, r1, r_min

        carry = lax.while_loop(cond, body, (i - 2, acc, r0, r1, r_min))
        o_ref[0, pl.ds(q_start, t), :] = carry[1].astype(o_ref.dtype)
        return 0

    lax.fori_loop(0, nq, tile, 0)


def _attention(qkv, bsz, seq):
    d = qkv.shape[-1] // 3
    qkv = qkv.reshape(bsz, seq, 3 * d)
    t = ATT_T
    npair = d // LANES
    return pl.pallas_call(
        _attention_kernel,
        grid=(bsz, npair),
        in_specs=[pl.BlockSpec((1, seq, LANES), lambda b, p: (b, 0, p)),
                  pl.BlockSpec((1, seq, LANES), lambda b, p: (b, 0, npair + p)),
                  pl.BlockSpec((1, seq, LANES), lambda b, p: (b, 0, 2 * npair + p))],
        out_specs=pl.BlockSpec((1, seq, LANES), lambda b, p: (b, 0, p)),
        out_shape=jax.ShapeDtypeStruct((bsz, seq, d), BF16),
        compiler_params=pltpu.CompilerParams(
            dimension_semantics=("arbitrary", "arbitrary"), vmem_limit_bytes=VMEM_LIMIT),
        name="sb_attention",
    )(qkv, qkv, qkv)


def _proj_ln_kernel(y_ref, x_ref, w_ref, ln_g_ref, ln_b_ref, o_ref):
    mix = jnp.dot(y_ref[...], w_ref[...], preferred_element_type=F32)
    o_ref[...] = _layer_norm(DN_ALPHA * x_ref[...] + mix, ln_g_ref[...], ln_b_ref[...])


def _proj_ln(y, xt, w, ln_g, ln_b):
    t, d = xt.shape
    tm = PROJ_TM
    return pl.pallas_call(
        _proj_ln_kernel,
        grid=(t // tm,),
        in_specs=[pl.BlockSpec((tm, d), lambda i: (i, 0)),
                  pl.BlockSpec((tm, d), lambda i: (i, 0)),
                  pl.BlockSpec((d, d), lambda i: (0, 0)),
                  pl.BlockSpec((1, d), lambda i: (0, 0)),
                  pl.BlockSpec((1, d), lambda i: (0, 0))],
        out_specs=pl.BlockSpec((tm, d), lambda i: (i, 0)),
        out_shape=jax.ShapeDtypeStruct((t, d), F32),
        compiler_params=pltpu.CompilerParams(
            dimension_semantics=("arbitrary",), vmem_limit_bytes=VMEM_LIMIT),
        name="out_proj_ln",
    )(y, xt, w.astype(BF16), ln_g.reshape(1, -1), ln_b.reshape(1, -1))


def _sb_mixer(x, w_qkv, w_out, ln_g, ln_b):
    bsz, seq, d = x.shape
    xt = x.reshape(-1, d)
    o = _attention(_qkv(xt, w_qkv), bsz, seq)
    return _proj_ln(o.reshape(-1, d), xt, w_out, ln_g, ln_b).reshape(bsz, seq, d)


def kernel(x, even_w_in, even_conv_w, even_sgu_ln_g, even_sgu_ln_b, even_sgu_w_s, even_sgu_b_s, even_w_out, odd_w_qkv, odd_w_out, mix_ln_g, mix_ln_b, moe_w_group, moe_b_group, moe_w_router, moe_b_router, moe_w1, moe_w3, moe_w2, ffn_ln_g, ffn_ln_b):
    for layer in range(DEPTH):
        i = layer // 2
        if layer % 2 == 0:
            x = _mixer0(x, even_w_in[i], even_conv_w[i], even_sgu_ln_g[i], even_sgu_ln_b[i],
                        even_sgu_w_s[i], even_sgu_b_s[i], even_w_out[i], mix_ln_g[layer], mix_ln_b[layer])
        else:
            x = _sb_mixer(x, odd_w_qkv[i], odd_w_out[i], mix_ln_g[layer], mix_ln_b[layer])
        x = _moe_layer(x, moe_w_group[layer], moe_b_group[layer], moe_w_router[layer],
                       moe_b_router[layer], moe_w1[layer], moe_w3[layer], moe_w2[layer],
                       ffn_ln_g[layer], ffn_ln_b[layer])
    return x
```

```python
import functools

import jax
import jax.numpy as jnp
from jax import lax
from jax.experimental import pallas as pl
from jax.experimental.pallas import tpu as pltpu

F32 = jnp.float32
BF16 = jnp.bfloat16

DEPTH = 2
DN_ALPHA = (2 * DEPTH) ** 0.25
LN_EPS = 1e-5

CONV_WIDTH = 512
CONV_K = 3
SGU_WIDTH = 512
SGU_HEADS = 8
SGU_HEAD_DIM = 64
SGU_BLOCK = 128
CHUNK = 64
SB_HEADS = 16
SB_HEAD_DIM = 64
N_GROUPS = 4
EXPERTS_PER_GROUP = 4
N_EXPERTS = N_GROUPS * EXPERTS_PER_GROUP

LANES = 128
SUBLANES = 8
VMEM_LIMIT = 56 * 1024 * 1024

MIX_TM = 512
PROJ_TM = 512
ROUTE_TM = 1024
ROUTE_ROWS = 8
PLAN_COLS = 16
MOE_TR = 256
GATHER_SLOTS = 3
UNSORT_TM = 256
ATT_T = 256
LOG2_E = 1.4426950408889634
ATT_DEAD = 150.0


def _layer_norm(r, g, b):
    mu = jnp.mean(r, axis=-1, keepdims=True)
    c = r - mu
    var = jnp.mean(c * c, axis=-1, keepdims=True)
    return c * lax.rsqrt(var + LN_EPS) * g + b


def _gelu_tanh(x):
    return 0.5 * x * (1.0 + jnp.tanh(0.7978845608028654 * (x + 0.044715 * (x * x * x))))


def _store_row_tiles(ref, val):
    n, d = val.shape
    assert d == SUBLANES * LANES
    for c in range(SUBLANES):
        ref[pl.ds(c, n, stride=SUBLANES), :] = val[:, c * LANES:(c + 1) * LANES]


def _load_row_tiles(ref, n):
    return jnp.concatenate([ref[pl.ds(c, n, stride=SUBLANES), :] for c in range(SUBLANES)], axis=1)


def _mixer0_kernel(x_ref, w_in_ref, conv_w_ref, sgu_g_ref, sgu_b_ref, ws_ref, bs_ref, w_out_ref,
                   ln_g_ref, ln_b_ref, o_ref, tail_ref, y_ref):
    tm = x_ref.shape[1]

    @pl.when(pl.program_id(1) == 0)
    def _():
        tail_ref[...] = jnp.zeros_like(tail_ref)

    x = x_ref[0]
    xb = x.astype(BF16)
    cw = CONV_WIDTH
    bch = jnp.dot(xb, w_in_ref[:, :3 * cw], preferred_element_type=F32)
    b_gate, c_gate, h = bch[:, :cw], bch[:, cw:2 * cw], bch[:, 2 * cw:]
    u = c_gate * h
    cat = jnp.concatenate([tail_ref[...], u], axis=0)
    tail_ref[...] = u[tm - 8:, :]
    u1 = cat[7:7 + tm, :]
    u2 = cat[6:6 + tm, :]
    conv = conv_w_ref[0:1, :] * u2 + conv_w_ref[1:2, :] * u1 + conv_w_ref[2:3, :] * u
    y_ref[:, :cw] = (b_gate * conv).astype(BF16)

    z = _gelu_tanh(jnp.dot(xb, w_in_ref[:, 3 * cw:], preferred_element_type=F32))
    z_u, z_v = z[:, :SGU_WIDTH], z[:, SGU_WIDTH:]
    v = _layer_norm(z_v, sgu_g_ref[...], sgu_b_ref[...])
    lane = lax.broadcasted_iota(jnp.int32, (tm, LANES), 1)
    first_head = lane < SGU_HEAD_DIM
    nblk = tm // SGU_BLOCK
    for j in range(SGU_HEADS // 2):
        vp = v[:, j * LANES:(j + 1) * LANES]
        v_a = jnp.where(first_head, vp, 0.0).astype(BF16)
        v_b = jnp.where(first_head, 0.0, vp).astype(BF16)
        rhs = jnp.concatenate(
            [jnp.concatenate([v_a[n * SGU_BLOCK:(n + 1) * SGU_BLOCK],
                              v_b[n * SGU_BLOCK:(n + 1) * SGU_BLOCK]], axis=0)
             for n in range(nblk)], axis=1)
        mixed = jnp.dot(ws_ref[j], rhs, preferred_element_type=F32)
        bias = bs_ref[:, j * LANES:(j + 1) * LANES]
        for n in range(nblk):
            rows = slice(n * SGU_BLOCK, (n + 1) * SGU_BLOCK)
            gate = z_u[rows, j * LANES:(j + 1) * LANES]
            y_ref[rows, cw + j * LANES:cw + (j + 1) * LANES] = (
                gate * (mixed[:, n * LANES:(n + 1) * LANES] + bias)).astype(BF16)

    mix = jnp.dot(y_ref[...], w_out_ref[...], preferred_element_type=F32)
    _store_row_tiles(o_ref, _layer_norm(DN_ALPHA * x + mix, ln_g_ref[...], ln_b_ref[...]))


def _mixer0(x, w_in, conv_w, sgu_g, sgu_b, w_s, b_s, w_out, ln_g, ln_b):
    bsz, seq, d = x.shape
    tm = MIX_TM
    pos = jnp.arange(SGU_BLOCK)
    chunk_causal = (pos[None, :] // CHUNK) <= (pos[:, None] // CHUNK)
    w_masked = jnp.where(chunk_causal[None], w_s, 0.0).astype(BF16)
    ws_pairs = jnp.concatenate([w_masked[0::2], w_masked[1::2]], axis=2)
    bias = jnp.repeat(jnp.transpose(b_s), SGU_HEAD_DIM, axis=1)
    full = lambda shape: pl.BlockSpec(shape, lambda b, s: (0,) * len(shape))
    return pl.pallas_call(
        _mixer0_kernel,
        grid=(bsz, seq // tm),
        in_specs=[
            pl.BlockSpec((1, tm, d), lambda b, s: (b, s, 0)),
            full(w_in.shape), full(conv_w.shape), full((1, SGU_WIDTH)), full((1, SGU_WIDTH)),
            full(ws_pairs.shape), full(bias.shape), full(w_out.shape),
            full((1, d)), full((1, d)),
        ],
        out_specs=pl.BlockSpec((tm * SUBLANES, LANES), lambda b, s: (b * (seq // tm) + s, 0)),
        out_shape=jax.ShapeDtypeStruct((bsz * seq * SUBLANES, LANES), F32),
        scratch_shapes=[pltpu.VMEM((8, CONV_WIDTH), F32), pltpu.VMEM((tm, d), BF16)],
        compiler_params=pltpu.CompilerParams(
            dimension_semantics=("arbitrary", "arbitrary"), vmem_limit_bytes=VMEM_LIMIT),
        name="mixer0",
    )(x, w_in.astype(BF16), conv_w, sgu_g.reshape(1, -1), sgu_b.reshape(1, -1), ws_pairs, bias,
      w_out.astype(BF16), ln_g.reshape(1, -1), ln_b.reshape(1, -1))


def _router_kernel(x_ref, w_ref, b_ref, route_ref):
    x = _load_row_tiles(x_ref, x_ref.shape[0] // SUBLANES)
    xh = x.astype(BF16)
    xl = (x - xh.astype(F32)).astype(BF16)
    w = w_ref[...]
    wh = w.astype(BF16)
    wl = (w - wh.astype(F32)).astype(BF16)
    logits_t = (jnp.dot(xh, wh, preferred_element_type=F32)
                + jnp.dot(xl, wh, preferred_element_type=F32)
                + jnp.dot(xh, wl, preferred_element_type=F32))
    logits = jnp.transpose(logits_t)[:b_ref.shape[0], :] + b_ref[...]
    gl = [logits[g:g + 1, :] for g in range(N_GROUPS)]
    g_max = functools.reduce(jnp.maximum, gl)
    g_top = 1.0 / functools.reduce(lambda a, b: a + b, [jnp.exp(l - g_max) for l in gl])
    g_idx = jnp.full_like(g_max, N_GROUPS).astype(jnp.int32)
    for g in reversed(range(N_GROUPS)):
        g_idx = jnp.where(gl[g] == g_max, g, g_idx)
    el = []
    for e in range(EXPERTS_PER_GROUP):
        acc = jnp.zeros_like(g_max)
        for g in range(N_GROUPS):
            r = N_GROUPS + g * EXPERTS_PER_GROUP + e
            acc = jnp.where(g_idx == g, logits[r:r + 1, :], acc)
        el.append(acc)
    m1 = functools.reduce(jnp.maximum, el)
    i1 = jnp.full_like(g_idx, EXPERTS_PER_GROUP)
    for e in reversed(range(EXPERTS_PER_GROUP)):
        i1 = jnp.where(el[e] == m1, e, i1)
    rest = [jnp.where(i1 == e, -jnp.inf, el[e]) for e in range(EXPERTS_PER_GROUP)]
    m2 = functools.reduce(jnp.maximum, rest)
    i2 = jnp.full_like(g_idx, EXPERTS_PER_GROUP)
    for e in reversed(range(EXPERTS_PER_GROUP)):
        i2 = jnp.where(rest[e] == m2, e, i2)
    p2 = jnp.exp(m2 - m1)
    w1 = g_top / (1.0 + p2)
    w2 = g_top * p2 / (1.0 + p2)
    for g in range(N_GROUPS):
        route_ref[g:g + 1, :] = jnp.where(g_idx == g, 1.0, 0.0)
    for e in range(EXPERTS_PER_GROUP):
        route_ref[N_GROUPS + e:N_GROUPS + e + 1, :] = jnp.where(
            i1 == e, w1, jnp.where(i2 == e, w2, 0.0))


def _router(x3, w_group, b_group, w_router, b_router):
    d = SUBLANES * LANES
    t = x3.shape[0] // SUBLANES
    tm = ROUTE_TM
    rows = 32
    w_all = jnp.concatenate(
        [jnp.transpose(w_group), jnp.transpose(w_router, (0, 2, 1)).reshape(N_EXPERTS, d)], axis=0)
    w_t = jnp.transpose(jnp.pad(w_all, ((0, LANES - w_all.shape[0]), (0, 0))))
    b_all = jnp.pad(jnp.concatenate([b_group, b_router.reshape(-1)]), (0, rows - N_GROUPS - N_EXPERTS))
    return pl.pallas_call(
        _router_kernel,
        grid=(t // tm,),
        in_specs=[pl.BlockSpec((tm * SUBLANES, LANES), lambda i: (i, 0)),
                  pl.BlockSpec((d, LANES), lambda i: (0, 0)),
                  pl.BlockSpec((rows, 1), lambda i: (0, 0))],
        out_specs=pl.BlockSpec((ROUTE_ROWS, tm), lambda i: (0, i)),
        out_shape=jax.ShapeDtypeStruct((ROUTE_ROWS, t), F32),
        compiler_params=pltpu.CompilerParams(
            dimension_semantics=("arbitrary",), vmem_limit_bytes=VMEM_LIMIT),
        name="router",
    )(x3, w_t, b_all.reshape(rows, 1))


def _plan_kernel(route_ref, ut_ref, plan_ref, src_ref, dst_ref, tg_ref, cum_s, m_s, vals_s, cin_s):
    t = route_ref.shape[1]
    rt = ut_ref.shape[0]
    nk = t // rt
    nt, tr, _ = plan_ref.shape
    lane = lax.broadcasted_iota(jnp.int32, (1, rt), 1)

    def cum_body(k, carry):
        off = pl.multiple_of(k * rt, rt)
        route = route_ref[:, pl.ds(off, rt)]
        cum = jnp.dot(route.astype(BF16), ut_ref[...], preferred_element_type=F32) + carry
        comb = route[N_GROUPS:, :]
        b0 = comb.astype(BF16).astype(F32)
        b1 = (comb - b0).astype(BF16).astype(F32)
        b2 = (comb - b0 - b1).astype(BF16).astype(F32)
        tok = off + lane
        pieces = ([b0[e:e + 1] for e in range(EXPERTS_PER_GROUP)]
                  + [b1[e:e + 1] for e in range(EXPERTS_PER_GROUP)]
                  + [b2[e:e + 1] for e in range(EXPERTS_PER_GROUP)]
                  + [(tok >> 8).astype(F32), (tok & 255).astype(F32)])
        for r, piece in enumerate(pieces):
            vals_s[r:r + 1, pl.ds(off, rt)] = piece
        for r in range(len(pieces), PLAN_COLS):
            vals_s[r:r + 1, pl.ds(off, rt)] = jnp.zeros((1, rt), F32)
        for g in range(N_GROUPS):
            cum_s[g, :, pl.ds(off, rt)] = cum[g:g + 1, :]
            m_s[g, :, pl.ds(off, rt)] = route[g:g + 1, :]
            cin_s[k * N_GROUPS + g] = carry[g, 0].astype(jnp.int32)
        return cum[:, rt - 1:rt]

    total = lax.fori_loop(0, nk, cum_body, jnp.zeros((ROUTE_ROWS, 1), F32))
    tile_base = [jnp.int32(0)]
    for g in range(N_GROUPS):
        cnt = total[g, 0].astype(jnp.int32)
        cin_s[nk * N_GROUPS + g] = cnt
        tile_base.append(tile_base[-1] + (cnt + tr - 1) // tr)

    def dst_body(k, _):
        off = pl.multiple_of(k * rt, rt)
        pos = jnp.zeros((1, rt), F32)
        for g in range(N_GROUPS):
            pos = pos + m_s[g, :, pl.ds(off, rt)] * (
                cum_s[g, :, pl.ds(off, rt)] + (tile_base[g] * tr - 1).astype(F32))
        dst_ref[:, pl.ds(off, rt)] = (pos * SUBLANES).astype(jnp.int32)
        return 0

    lax.fori_loop(0, nk, dst_body, 0)

    row_iota = lax.broadcasted_iota(jnp.int32, (tr, rt), 0).astype(F32)

    def tile_body(i, k_prev):
        g = ((i >= tile_base[1]).astype(jnp.int32) + (i >= tile_base[2]).astype(jnp.int32)
             + (i >= tile_base[3]).astype(jnp.int32))
        base = jnp.where(g == 0, tile_base[0],
                         jnp.where(g == 1, tile_base[1], jnp.where(g == 2, tile_base[2], tile_base[3])))
        r0 = (i - base) * tr
        tg_ref[i] = g
        target = row_iota + (r0 + 1).astype(F32)

        k_first = lax.while_loop(
            lambda k: jnp.logical_and(k < nk, cin_s[jnp.minimum(k + 1, nk) * N_GROUPS + g] <= r0),
            lambda k: k + 1, jnp.where(i == base, 0, k_prev))

        def more(c):
            return jnp.logical_and(c[0] < nk, cin_s[c[0] * N_GROUPS + g] < r0 + tr)

        def pick(c):
            k, acc = c
            off = pl.multiple_of(k * rt, rt)
            onehot = jnp.where(cum_s[g, :, pl.ds(off, rt)] == target,
                               m_s[g, :, pl.ds(off, rt)], 0.0).astype(BF16)
            vals = vals_s[:, pl.ds(off, rt)].astype(BF16)
            return k + 1, acc + lax.dot_general(onehot, vals, (((1,), (1,)), ((), ())),
                                                preferred_element_type=F32)

        k_end, acc = lax.while_loop(more, pick, (k_first, jnp.zeros((tr, PLAN_COLS), F32)))
        plan_ref[i] = acc
        digits = jnp.transpose(jnp.concatenate(
            [acc, jnp.zeros((tr, LANES - PLAN_COLS), F32)], axis=1))
        n = 3 * EXPERTS_PER_GROUP
        src_ref[i] = ((digits[n:n + 1, :] * 256.0 + digits[n + 1:n + 2, :]) * SUBLANES).astype(jnp.int32)
        return jnp.maximum(k_end - 1, k_first)

    lax.fori_loop(0, nt, tile_body, jnp.int32(0))


def _plan(route):
    t = route.shape[1]
    rt = ROUTE_TM
    tr = MOE_TR
    nt = t // tr + N_GROUPS
    idx = jnp.arange(rt)
    ut = (idx[:, None] <= idx[None, :]).astype(BF16)
    return pl.pallas_call(
        _plan_kernel,
        in_specs=[pl.BlockSpec(memory_space=pltpu.VMEM), pl.BlockSpec(memory_space=pltpu.VMEM)],
        out_specs=[pl.BlockSpec(memory_space=pltpu.VMEM), pl.BlockSpec(memory_space=pltpu.VMEM),
                   pl.BlockSpec(memory_space=pltpu.VMEM), pl.BlockSpec(memory_space=pltpu.SMEM)],
        out_shape=[jax.ShapeDtypeStruct((nt, tr, PLAN_COLS), F32),
                   jax.ShapeDtypeStruct((nt, 1, tr), jnp.int32),
                   jax.ShapeDtypeStruct((1, t), jnp.int32),
                   jax.ShapeDtypeStruct((nt,), jnp.int32)],
        scratch_shapes=[pltpu.VMEM((N_GROUPS, 1, t), F32), pltpu.VMEM((N_GROUPS, 1, t), F32),
                        pltpu.VMEM((PLAN_COLS, t), F32),
                        pltpu.SMEM(((t // rt + 1) * N_GROUPS,), jnp.int32)],
        compiler_params=pltpu.CompilerParams(vmem_limit_bytes=VMEM_LIMIT),
        name="moe_plan",
    )(route, ut)


def _start_row_gather(idx_ref, first, src_ref, buf, slot, sem):
    if not isinstance(slot, int):
        for s in range(buf.shape[0]):
            @pl.when(slot == s)
            def _():
                _start_row_gather(idx_ref, first, src_ref, buf, s, sem)
        return
    for r in range(buf.shape[1] // SUBLANES):
        row = pl.multiple_of(idx_ref[first + r], SUBLANES)
        pltpu.make_async_copy(src_ref.at[pl.ds(row, SUBLANES), :],
                              buf.at[slot, pl.ds(r * SUBLANES, SUBLANES), :], sem.at[slot]).start()


def _wait_row_gather(src_ref, buf, slot, sem):
    pltpu.make_async_copy(src_ref.at[pl.ds(0, buf.shape[1]), :], buf.at[slot], sem.at[slot]).wait()


def _ffn_kernel(tg_ref, src_ref, x_ref, plan_ref, w1_ref, w3_ref, w2_ref, ln_g_ref, ln_b_ref,
                o_ref, xbuf, sem):
    i = pl.program_id(0)
    last = pl.num_programs(0) - 1
    nslot = xbuf.shape[0]
    tr = xbuf.shape[1] // SUBLANES
    slot = i % nslot

    @pl.when(i == 0)
    def _():
        for ahead in range(nslot - 1):
            _start_row_gather(src_ref, jnp.minimum(ahead, last) * tr, x_ref, xbuf, ahead, sem)

    _wait_row_gather(x_ref, xbuf, slot, sem)
    x = _load_row_tiles(xbuf.at[slot], tr)
    xb = x.astype(BF16)
    plan = plan_ref[0]
    y = jnp.zeros(x.shape, F32)
    for e in range(EXPERTS_PER_GROUP):
        n = EXPERTS_PER_GROUP
        c = plan[:, e:e + 1] + plan[:, n + e:n + e + 1] + plan[:, 2 * n + e:2 * n + e + 1]
        h1 = jnp.dot(xb, w1_ref[0, e], preferred_element_type=F32)
        h3 = jnp.dot(xb, w3_ref[0, e], preferred_element_type=F32)
        hid = (h1 * jax.nn.sigmoid(h1)) * h3 * c
        y = y + jnp.dot(hid.astype(BF16), w2_ref[0, e], preferred_element_type=F32)
    _store_row_tiles(o_ref, _layer_norm(DN_ALPHA * x + y, ln_g_ref[...], ln_b_ref[...]))

    _start_row_gather(src_ref, jnp.minimum(i + nslot - 1, last) * tr, x_ref, xbuf,
                      (i + nslot - 1) % nslot, sem)

    @pl.when(i == last)
    def _():
        for ahead in range(1, nslot):
            _wait_row_gather(x_ref, xbuf, (i + ahead) % nslot, sem)


def _ffn(tile_group, src, x3, plan, w1, w3, w2, ln_g, ln_b):
    nt, tr, _ = plan.shape
    d, f = w1.shape[-2:]
    n = EXPERTS_PER_GROUP
    wspec = lambda shape: pl.BlockSpec((1,) + shape, lambda i, tg, src: (tg[i], 0, 0, 0))
    return pl.pallas_call(
        _ffn_kernel,
        grid_spec=pltpu.PrefetchScalarGridSpec(
            num_scalar_prefetch=2,
            grid=(nt,),
            in_specs=[pl.BlockSpec(memory_space=pl.ANY),
                      pl.BlockSpec((1, tr, PLAN_COLS), lambda i, tg, src: (i, 0, 0)),
                      wspec((n, d, f)), wspec((n, d, f)), wspec((n, f, d)),
                      pl.BlockSpec((1, d), lambda i, tg, src: (0, 0)),
                      pl.BlockSpec((1, d), lambda i, tg, src: (0, 0))],
            out_specs=pl.BlockSpec((tr * SUBLANES, LANES), lambda i, tg, src: (i, 0)),
            scratch_shapes=[pltpu.VMEM((GATHER_SLOTS, tr * SUBLANES, LANES), F32),
                            pltpu.SemaphoreType.DMA((GATHER_SLOTS,))]),
        out_shape=jax.ShapeDtypeStruct((nt * tr * SUBLANES, LANES), F32),
        compiler_params=pltpu.CompilerParams(
            dimension_semantics=("arbitrary",), vmem_limit_bytes=VMEM_LIMIT),
        name="moe_ffn",
    )(tile_group, src, x3, plan, w1.astype(BF16), w3.astype(BF16), w2.astype(BF16),
      ln_g.reshape(1, -1), ln_b.reshape(1, -1))


def _unsort_kernel(dst_ref, y_ref, o_ref, buf, sem):
    i = pl.program_id(0)
    last = pl.num_programs(0) - 1
    nslot = buf.shape[0]
    tm = buf.shape[1] // SUBLANES
    slot = i % nslot

    @pl.when(i == 0)
    def _():
        for ahead in range(nslot - 1):
            _start_row_gather(dst_ref, jnp.minimum(ahead, last) * tm, y_ref, buf, ahead, sem)

    _wait_row_gather(y_ref, buf, slot, sem)
    o_ref[...] = _load_row_tiles(buf.at[slot], tm)
    _start_row_gather(dst_ref, jnp.minimum(i + nslot - 1, last) * tm, y_ref, buf,
                      (i + nslot - 1) % nslot, sem)

    @pl.when(i == last)
    def _():
        for ahead in range(1, nslot):
            _wait_row_gather(y_ref, buf, (i + ahead) % nslot, sem)


def _unsort(dst, y3, t):
    d = SUBLANES * LANES
    tm = UNSORT_TM
    return pl.pallas_call(
        _unsort_kernel,
        grid_spec=pltpu.PrefetchScalarGridSpec(
            num_scalar_prefetch=1,
            grid=(t // tm,),
            in_specs=[pl.BlockSpec(memory_space=pl.ANY)],
            out_specs=pl.BlockSpec((tm, d), lambda i, dst: (i, 0)),
            scratch_shapes=[pltpu.VMEM((GATHER_SLOTS, tm * SUBLANES, LANES), F32),
                            pltpu.SemaphoreType.DMA((GATHER_SLOTS,))]),
        out_shape=jax.ShapeDtypeStruct((t, d), F32),
        compiler_params=pltpu.CompilerParams(
            dimension_semantics=("arbitrary",), vmem_limit_bytes=VMEM_LIMIT),
        name="moe_unsort",
    )(dst, y3)


def _moe_layer(x3, w_group, b_group, w_router, b_router, w1, w3, w2, ln_g, ln_b):
    route = _router(x3, w_group, b_group, w_router, b_router)
    plan, src, dst, tile_group = _plan(route)
    y3 = _ffn(tile_group, src.reshape(-1), x3, plan, w1, w3, w2, ln_g, ln_b)
    return _unsort(dst.reshape(-1), y3, x3.shape[0] // SUBLANES)


def _qkv_kernel(x_ref, w_ref, o_ref):
    d = x_ref.shape[1]
    xb = x_ref[...].astype(BF16)
    for part in range(3):
        y = jnp.dot(xb, w_ref[:, part * d:(part + 1) * d], preferred_element_type=F32)
        if part == 0:
            y = y * (SB_HEAD_DIM ** -0.5 * LOG2_E)
        o_ref[:, part * d:(part + 1) * d] = y.astype(BF16)


def _qkv(xt, w_qkv):
    t, d = xt.shape
    tm = PROJ_TM
    return pl.pallas_call(
        _qkv_kernel,
        grid=(t // tm,),
        in_specs=[pl.BlockSpec((tm, d), lambda i: (i, 0)),
                  pl.BlockSpec((d, 3 * d), lambda i: (0, 0))],
        out_specs=pl.BlockSpec((tm, 3 * d), lambda i: (i, 0)),
        out_shape=jax.ShapeDtypeStruct((t, 3 * d), BF16),
        compiler_params=pltpu.CompilerParams(
            dimension_semantics=("arbitrary",), vmem_limit_bytes=VMEM_LIMIT),
        name="qkv_proj",
    )(xt, w_qkv.astype(BF16))


def _softplus2(z2):
    return jnp.maximum(z2, 0.0) + jnp.log2(1.0 + jnp.exp2(-jnp.abs(z2)))


def _attention_kernel(q_ref, k_ref, v_ref, o_ref):
    t = ATT_T
    nq = q_ref.shape[1] // t
    nt_dims = (((1,), (1,)), ((), ()))
    lane = lax.broadcasted_iota(jnp.int32, (t, LANES), 1)
    first_head = lane < SB_HEAD_DIM
    row = lax.broadcasted_iota(jnp.int32, (t, t), 0)
    col = lax.broadcasted_iota(jnp.int32, (t, t), 1)
    tri = jnp.where(row >= col, 1.0, 0.0).astype(BF16)
    strict = col < row

    def tile(i, _):
        q_start = pl.multiple_of(i * t, t)
        p_start = pl.multiple_of(jnp.maximum(i - 1, 0) * t, t)
        has_prev = i > 0
        q = q_ref[0, pl.ds(q_start, t), :]
        q_heads = (jnp.where(first_head, q, jnp.zeros_like(q)),
                   jnp.where(first_head, jnp.zeros_like(q), q))
        keys = {"d": k_ref[0, pl.ds(q_start, t), :], "p": k_ref[0, pl.ds(p_start, t), :]}
        vals = {"d": v_ref[0, pl.ds(q_start, t), :], "p": v_ref[0, pl.ds(p_start, t), :]}

        chains = [(h, b) for b in ("d", "p") for h in range(2)]
        zs = {c: lax.dot_general(q_heads[c[0]], keys[c[1]], nt_dims, preferred_element_type=F32)
              for c in chains}
        sps, cs, atts, outs = {}, {}, {}, {}

        def keep(c, x):
            return jnp.where(strict, x, 0.0) if c[1] == "d" else jnp.where(has_prev, x, 0.0)

        def sp_of(c):
            return keep(c, _softplus2(zs[c])).astype(BF16)

        def att_of(c):
            e = zs[c] - cs[c]
            if c[1] == "p":
                e = e - cs[(c[0], "d")][:, 0:1]
            return keep(c, jnp.exp2(e)).astype(BF16)

        sps[chains[0]] = sp_of(chains[0])
        for n, c in enumerate(chains):
            cs[c] = jnp.dot(sps[c], tri, preferred_element_type=F32)
            if n + 1 < len(chains):
                sps[chains[n + 1]] = sp_of(chains[n + 1])
        r0, r1 = [cs[(h, "d")][:, 0:1] + cs[(h, "p")][:, 0:1] for h in range(2)]
        r_min = jnp.min(jnp.minimum(r0, r1))
        atts[chains[0]] = att_of(chains[0])
        for n, c in enumerate(chains):
            outs[c] = jnp.dot(atts[c], vals[c[1]], preferred_element_type=F32)
            if n + 1 < len(chains):
                atts[chains[n + 1]] = att_of(chains[n + 1])
        acc = jnp.where(first_head, outs[(0, "d")] + outs[(0, "p")], outs[(1, "d")] + outs[(1, "p")])

        def block(kb, acc, r0, r1):
            start = pl.multiple_of(kb * t, t)
            k = k_ref[0, pl.ds(start, t), :]
            v = v_ref[0, pl.ds(start, t), :]
            z = [lax.dot_general(qh, k, nt_dims, preferred_element_type=F32) for qh in q_heads]
            sp0 = _softplus2(z[0]).astype(BF16)
            c0 = jnp.dot(sp0, tri, preferred_element_type=F32)
            sp1 = _softplus2(z[1]).astype(BF16)
            c1 = jnp.dot(sp1, tri, preferred_element_type=F32)
            att0 = jnp.exp2(z[0] - c0 - r0).astype(BF16)
            o0 = jnp.dot(att0, v, preferred_element_type=F32)
            att1 = jnp.exp2(z[1] - c1 - r1).astype(BF16)
            o1 = jnp.dot(att1, v, preferred_element_type=F32)
            r0n = r0 + c0[:, 0:1]
            r1n = r1 + c1[:, 0:1]
            return (acc + jnp.where(first_head, o0, o1), r0n, r1n,
                    jnp.min(jnp.minimum(r0n, r1n)))

        def cond(c):
            return jnp.logical_and(c[0] >= 0, c[4] < ATT_DEAD)

        def body(c):
            kb, acc, r0, r1, _ = c
            acc, r0, r1, r_min = block(kb, acc, r0, r1)
            return kb - 1, acc, r0, r1, r_min

        carry = lax.while_loop(cond, body, (i - 2, acc, r0, r1, r_min))
        o_ref[0, pl.ds(q_start, t), :] = carry[1].astype(o_ref.dtype)
        return 0

    lax.fori_loop(0, nq, tile, 0)


def _attention(qkv, bsz, seq):
    d = qkv.shape[-1] // 3
    qkv = qkv.reshape(bsz, seq, 3 * d)
    t = ATT_T
    npair = d // LANES
    return pl.pallas_call(
        _attention_kernel,
        grid=(bsz, npair),
        in_specs=[pl.BlockSpec((1, seq, LANES), lambda b, p: (b, 0, p)),
                  pl.BlockSpec((1, seq, LANES), lambda b, p: (b, 0, npair + p)),
                  pl.BlockSpec((1, seq, LANES), lambda b, p: (b, 0, 2 * npair + p))],
        out_specs=pl.BlockSpec((1, seq, LANES), lambda b, p: (b, 0, p)),
        out_shape=jax.ShapeDtypeStruct((bsz, seq, d), BF16),
        compiler_params=pltpu.CompilerParams(
            dimension_semantics=("arbitrary", "arbitrary"), vmem_limit_bytes=VMEM_LIMIT),
        name="sb_attention",
    )(qkv, qkv, qkv)


def _proj_ln_kernel(y_ref, x_ref, w_ref, ln_g_ref, ln_b_ref, o_ref):
    mix = jnp.dot(y_ref[...], w_ref[...], preferred_element_type=F32)
    _store_row_tiles(o_ref, _layer_norm(DN_ALPHA * x_ref[...] + mix, ln_g_ref[...], ln_b_ref[...]))


def _proj_ln(y, xt, w, ln_g, ln_b):
    t, d = xt.shape
    tm = PROJ_TM
    return pl.pallas_call(
        _proj_ln_kernel,
        grid=(t // tm,),
        in_specs=[pl.BlockSpec((tm, d), lambda i: (i, 0)),
                  pl.BlockSpec((tm, d), lambda i: (i, 0)),
                  pl.BlockSpec((d, d), lambda i: (0, 0)),
                  pl.BlockSpec((1, d), lambda i: (0, 0)),
                  pl.BlockSpec((1, d), lambda i: (0, 0))],
        out_specs=pl.BlockSpec((tm * SUBLANES, LANES), lambda i: (i, 0)),
        out_shape=jax.ShapeDtypeStruct((t * SUBLANES, LANES), F32),
        compiler_params=pltpu.CompilerParams(
            dimension_semantics=("arbitrary",), vmem_limit_bytes=VMEM_LIMIT),
        name="out_proj_ln",
    )(y, xt, w.astype(BF16), ln_g.reshape(1, -1), ln_b.reshape(1, -1))


def _sb_mixer(x, w_qkv, w_out, ln_g, ln_b):
    bsz, seq, d = x.shape
    xt = x.reshape(-1, d)
    o = _attention(_qkv(xt, w_qkv), bsz, seq)
    return _proj_ln(o.reshape(-1, d), xt, w_out, ln_g, ln_b)


def kernel(x, even_w_in, even_conv_w, even_sgu_ln_g, even_sgu_ln_b, even_sgu_w_s, even_sgu_b_s, even_w_out, odd_w_qkv, odd_w_out, mix_ln_g, mix_ln_b, moe_w_group, moe_b_group, moe_w_router, moe_b_router, moe_w1, moe_w3, moe_w2, ffn_ln_g, ffn_ln_b):
    shape = x.shape
    for layer in range(DEPTH):
        i = layer // 2
        if layer % 2 == 0:
            x3 = _mixer0(x, even_w_in[i], even_conv_w[i], even_sgu_ln_g[i], even_sgu_ln_b[i],
                         even_sgu_w_s[i], even_sgu_b_s[i], even_w_out[i], mix_ln_g[layer], mix_ln_b[layer])
        else:
            x3 = _sb_mixer(x, odd_w_qkv[i], odd_w_out[i], mix_ln_g[layer], mix_ln_b[layer])
        x = _moe_layer(x3, moe_w_group[layer], moe_b_group[layer], moe_w_router[layer],
                       moe_b_router[layer], moe_w1[layer], moe_w3[layer], moe_w2[layer],
                       ffn_ln_g[layer], ffn_ln_b[layer]).reshape(shape)
    return x
```

```python
import functools

import jax
import jax.numpy as jnp
from jax import lax
from jax.experimental import pallas as pl
from jax.experimental.pallas import tpu as pltpu

F32 = jnp.float32
BF16 = jnp.bfloat16

DEPTH = 2
DN_ALPHA = (2 * DEPTH) ** 0.25
LN_EPS = 1e-5

CONV_WIDTH = 512
CONV_K = 3
SGU_WIDTH = 512
SGU_HEADS = 8
SGU_HEAD_DIM = 64
SGU_BLOCK = 128
CHUNK = 64
SB_HEADS = 16
SB_HEAD_DIM = 64
N_GROUPS = 4
EXPERTS_PER_GROUP = 4
N_EXPERTS = N_GROUPS * EXPERTS_PER_GROUP

LANES = 128
SUBLANES = 8
VMEM_LIMIT = 56 * 1024 * 1024

MIX_TM = 512
PROJ_TM = 1024
ROUTE_TM = 1024
ROUTE_ROWS = 8
PLAN_COLS = 16
MOE_TR = 256
GATHER_SLOTS = 3
UNSORT_TM = 512
ATT_T = 256
LOG2_E = 1.4426950408889634
ATT_DEAD = 150.0


def _layer_norm(r, g, b):
    mu = jnp.mean(r, axis=-1, keepdims=True)
    c = r - mu
    var = jnp.mean(c * c, axis=-1, keepdims=True)
    return c * lax.rsqrt(var + LN_EPS) * g + b


def _gelu_tanh(x):
    return 0.5 * x * (1.0 + jnp.tanh(0.7978845608028654 * (x + 0.044715 * (x * x * x))))


def _store_row_tiles(ref, val):
    n, d = val.shape
    assert d == SUBLANES * LANES
    for c in range(SUBLANES):
        ref[pl.ds(c, n, stride=SUBLANES), :] = val[:, c * LANES:(c + 1) * LANES]


def _load_row_tiles(ref, n):
    return jnp.concatenate([ref[pl.ds(c, n, stride=SUBLANES), :] for c in range(SUBLANES)], axis=1)


def _mixer0_kernel(x_ref, w_in_ref, conv_w_ref, sgu_g_ref, sgu_b_ref, ws_ref, bs_ref, w_out_ref,
                   ln_g_ref, ln_b_ref, o_ref, tail_ref, y_ref):
    tm = x_ref.shape[1]

    @pl.when(pl.program_id(1) == 0)
    def _():
        tail_ref[...] = jnp.zeros_like(tail_ref)

    x = x_ref[0]
    xb = x.astype(BF16)
    cw = CONV_WIDTH
    bch = jnp.dot(xb, w_in_ref[:, :3 * cw], preferred_element_type=F32)
    b_gate, c_gate, h = bch[:, :cw], bch[:, cw:2 * cw], bch[:, 2 * cw:]
    u = c_gate * h
    cat = jnp.concatenate([tail_ref[...], u], axis=0)
    tail_ref[...] = u[tm - 8:, :]
    u1 = cat[7:7 + tm, :]
    u2 = cat[6:6 + tm, :]
    conv = conv_w_ref[0:1, :] * u2 + conv_w_ref[1:2, :] * u1 + conv_w_ref[2:3, :] * u
    y_ref[:, :cw] = (b_gate * conv).astype(BF16)

    z = _gelu_tanh(jnp.dot(xb, w_in_ref[:, 3 * cw:], preferred_element_type=F32))
    z_u, z_v = z[:, :SGU_WIDTH], z[:, SGU_WIDTH:]
    v = _layer_norm(z_v, sgu_g_ref[...], sgu_b_ref[...])
    lane = lax.broadcasted_iota(jnp.int32, (tm, LANES), 1)
    first_head = lane < SGU_HEAD_DIM
    nblk = tm // SGU_BLOCK
    for j in range(SGU_HEADS // 2):
        vp = v[:, j * LANES:(j + 1) * LANES]
        v_a = jnp.where(first_head, vp, 0.0).astype(BF16)
        v_b = jnp.where(first_head, 0.0, vp).astype(BF16)
        rhs = jnp.concatenate(
            [jnp.concatenate([v_a[n * SGU_BLOCK:(n + 1) * SGU_BLOCK],
                              v_b[n * SGU_BLOCK:(n + 1) * SGU_BLOCK]], axis=0)
             for n in range(nblk)], axis=1)
        mixed = jnp.dot(ws_ref[j], rhs, preferred_element_type=F32)
        bias = bs_ref[:, j * LANES:(j + 1) * LANES]
        for n in range(nblk):
            rows = slice(n * SGU_BLOCK, (n + 1) * SGU_BLOCK)
            gate = z_u[rows, j * LANES:(j + 1) * LANES]
            y_ref[rows, cw + j * LANES:cw + (j + 1) * LANES] = (
                gate * (mixed[:, n * LANES:(n + 1) * LANES] + bias)).astype(BF16)

    mix = jnp.dot(y_ref[...], w_out_ref[...], preferred_element_type=F32)
    _store_row_tiles(o_ref, _layer_norm(DN_ALPHA * x + mix, ln_g_ref[...], ln_b_ref[...]))


def _mixer0(x, w_in, conv_w, sgu_g, sgu_b, w_s, b_s, w_out, ln_g, ln_b):
    bsz, seq, d = x.shape
    tm = MIX_TM
    pos = jnp.arange(SGU_BLOCK)
    chunk_causal = (pos[None, :] // CHUNK) <= (pos[:, None] // CHUNK)
    w_masked = jnp.where(chunk_causal[None], w_s, 0.0).astype(BF16)
    ws_pairs = jnp.concatenate([w_masked[0::2], w_masked[1::2]], axis=2)
    bias = jnp.repeat(jnp.transpose(b_s), SGU_HEAD_DIM, axis=1)
    full = lambda shape: pl.BlockSpec(shape, lambda b, s: (0,) * len(shape))
    return pl.pallas_call(
        _mixer0_kernel,
        grid=(bsz, seq // tm),
        in_specs=[
            pl.BlockSpec((1, tm, d), lambda b, s: (b, s, 0)),
            full(w_in.shape), full(conv_w.shape), full((1, SGU_WIDTH)), full((1, SGU_WIDTH)),
            full(ws_pairs.shape), full(bias.shape), full(w_out.shape),
            full((1, d)), full((1, d)),
        ],
        out_specs=pl.BlockSpec((tm * SUBLANES, LANES), lambda b, s: (b * (seq // tm) + s, 0)),
        out_shape=jax.ShapeDtypeStruct((bsz * seq * SUBLANES, LANES), F32),
        scratch_shapes=[pltpu.VMEM((8, CONV_WIDTH), F32), pltpu.VMEM((tm, d), BF16)],
        compiler_params=pltpu.CompilerParams(
            dimension_semantics=("arbitrary", "arbitrary"), vmem_limit_bytes=VMEM_LIMIT),
        name="mixer0",
    )(x, w_in.astype(BF16), conv_w, sgu_g.reshape(1, -1), sgu_b.reshape(1, -1), ws_pairs, bias,
      w_out.astype(BF16), ln_g.reshape(1, -1), ln_b.reshape(1, -1))


def _router_kernel(x_ref, w_ref, b_ref, route_ref):
    x = _load_row_tiles(x_ref, x_ref.shape[0] // SUBLANES)
    xh = x.astype(BF16)
    xl = (x - xh.astype(F32)).astype(BF16)
    w = w_ref[...]
    wh = w.astype(BF16)
    wl = (w - wh.astype(F32)).astype(BF16)
    logits_t = (jnp.dot(xh, wh, preferred_element_type=F32)
                + jnp.dot(xl, wh, preferred_element_type=F32)
                + jnp.dot(xh, wl, preferred_element_type=F32))
    logits = jnp.transpose(logits_t)[:b_ref.shape[0], :] + b_ref[...]
    gl = [logits[g:g + 1, :] for g in range(N_GROUPS)]
    g_max = functools.reduce(jnp.maximum, gl)
    g_top = 1.0 / functools.reduce(lambda a, b: a + b, [jnp.exp(l - g_max) for l in gl])
    g_idx = jnp.full_like(g_max, N_GROUPS).astype(jnp.int32)
    for g in reversed(range(N_GROUPS)):
        g_idx = jnp.where(gl[g] == g_max, g, g_idx)
    el = []
    for e in range(EXPERTS_PER_GROUP):
        acc = jnp.zeros_like(g_max)
        for g in range(N_GROUPS):
            r = N_GROUPS + g * EXPERTS_PER_GROUP + e
            acc = jnp.where(g_idx == g, logits[r:r + 1, :], acc)
        el.append(acc)
    m1 = functools.reduce(jnp.maximum, el)
    i1 = jnp.full_like(g_idx, EXPERTS_PER_GROUP)
    for e in reversed(range(EXPERTS_PER_GROUP)):
        i1 = jnp.where(el[e] == m1, e, i1)
    rest = [jnp.where(i1 == e, -jnp.inf, el[e]) for e in range(EXPERTS_PER_GROUP)]
    m2 = functools.reduce(jnp.maximum, rest)
    i2 = jnp.full_like(g_idx, EXPERTS_PER_GROUP)
    for e in reversed(range(EXPERTS_PER_GROUP)):
        i2 = jnp.where(rest[e] == m2, e, i2)
    p2 = jnp.exp(m2 - m1)
    w1 = g_top / (1.0 + p2)
    w2 = g_top * p2 / (1.0 + p2)
    for g in range(N_GROUPS):
        route_ref[g:g + 1, :] = jnp.where(g_idx == g, 1.0, 0.0)
    for e in range(EXPERTS_PER_GROUP):
        route_ref[N_GROUPS + e:N_GROUPS + e + 1, :] = jnp.where(
            i1 == e, w1, jnp.where(i2 == e, w2, 0.0))


def _router(x3, w_group, b_group, w_router, b_router):
    d = SUBLANES * LANES
    t = x3.shape[0] // SUBLANES
    tm = ROUTE_TM
    rows = 32
    w_all = jnp.concatenate(
        [jnp.transpose(w_group), jnp.transpose(w_router, (0, 2, 1)).reshape(N_EXPERTS, d)], axis=0)
    w_t = jnp.transpose(jnp.pad(w_all, ((0, LANES - w_all.shape[0]), (0, 0))))
    b_all = jnp.pad(jnp.concatenate([b_group, b_router.reshape(-1)]), (0, rows - N_GROUPS - N_EXPERTS))
    return pl.pallas_call(
        _router_kernel,
        grid=(t // tm,),
        in_specs=[pl.BlockSpec((tm * SUBLANES, LANES), lambda i: (i, 0)),
                  pl.BlockSpec((d, LANES), lambda i: (0, 0)),
                  pl.BlockSpec((rows, 1), lambda i: (0, 0))],
        out_specs=pl.BlockSpec((ROUTE_ROWS, tm), lambda i: (0, i)),
        out_shape=jax.ShapeDtypeStruct((ROUTE_ROWS, t), F32),
        compiler_params=pltpu.CompilerParams(
            dimension_semantics=("arbitrary",), vmem_limit_bytes=VMEM_LIMIT),
        name="router",
    )(x3, w_t, b_all.reshape(rows, 1))


def _plan_kernel(route_ref, ut_ref, plan_ref, src_ref, dst_ref, tg_ref, cum_s, m_s, vals_s, cin_s):
    t = route_ref.shape[1]
    rt = ut_ref.shape[0]
    nk = t // rt
    nt, tr, _ = plan_ref.shape
    lane = lax.broadcasted_iota(jnp.int32, (1, rt), 1)

    def cum_body(k, carry):
        off = pl.multiple_of(k * rt, rt)
        route = route_ref[:, pl.ds(off, rt)]
        cum = jnp.dot(route.astype(BF16), ut_ref[...], preferred_element_type=F32) + carry
        comb = route[N_GROUPS:, :]
        b0 = comb.astype(BF16).astype(F32)
        b1 = (comb - b0).astype(BF16).astype(F32)
        b2 = (comb - b0 - b1).astype(BF16).astype(F32)
        tok = off + lane
        pieces = ([b0[e:e + 1] for e in range(EXPERTS_PER_GROUP)]
                  + [b1[e:e + 1] for e in range(EXPERTS_PER_GROUP)]
                  + [b2[e:e + 1] for e in range(EXPERTS_PER_GROUP)]
                  + [(tok >> 8).astype(F32), (tok & 255).astype(F32)])
        for r, piece in enumerate(pieces):
            vals_s[r:r + 1, pl.ds(off, rt)] = piece
        for r in range(len(pieces), PLAN_COLS):
            vals_s[r:r + 1, pl.ds(off, rt)] = jnp.zeros((1, rt), F32)
        for g in range(N_GROUPS):
            cum_s[g, :, pl.ds(off, rt)] = cum[g:g + 1, :]
            m_s[g, :, pl.ds(off, rt)] = route[g:g + 1, :]
            cin_s[k * N_GROUPS + g] = carry[g, 0].astype(jnp.int32)
        return cum[:, rt - 1:rt]

    total = lax.fori_loop(0, nk, cum_body, jnp.zeros((ROUTE_ROWS, 1), F32))
    tile_base = [jnp.int32(0)]
    for g in range(N_GROUPS):
        cnt = total[g, 0].astype(jnp.int32)
        cin_s[nk * N_GROUPS + g] = cnt
        tile_base.append(tile_base[-1] + (cnt + tr - 1) // tr)

    def dst_body(k, _):
        off = pl.multiple_of(k * rt, rt)
        pos = jnp.zeros((1, rt), F32)
        for g in range(N_GROUPS):
            pos = pos + m_s[g, :, pl.ds(off, rt)] * (
                cum_s[g, :, pl.ds(off, rt)] + (tile_base[g] * tr - 1).astype(F32))
        dst_ref[:, pl.ds(off, rt)] = (pos * SUBLANES).astype(jnp.int32)
        return 0

    lax.fori_loop(0, nk, dst_body, 0)

    row_iota = lax.broadcasted_iota(jnp.int32, (tr, rt), 0).astype(F32)

    def tile_body(i, k_prev):
        g = ((i >= tile_base[1]).astype(jnp.int32) + (i >= tile_base[2]).astype(jnp.int32)
             + (i >= tile_base[3]).astype(jnp.int32))
        base = jnp.where(g == 0, tile_base[0],
                         jnp.where(g == 1, tile_base[1], jnp.where(g == 2, tile_base[2], tile_base[3])))
        r0 = (i - base) * tr
        tg_ref[i] = g
        target = row_iota + (r0 + 1).astype(F32)

        k_first = lax.while_loop(
            lambda k: jnp.logical_and(k < nk, cin_s[jnp.minimum(k + 1, nk) * N_GROUPS + g] <= r0),
            lambda k: k + 1, jnp.where(i == base, 0, k_prev))

        def more(c):
            return jnp.logical_and(c[0] < nk, cin_s[c[0] * N_GROUPS + g] < r0 + tr)

        def pick(c):
            k, acc = c
            off = pl.multiple_of(k * rt, rt)
            onehot = jnp.where(cum_s[g, :, pl.ds(off, rt)] == target,
                               m_s[g, :, pl.ds(off, rt)], 0.0).astype(BF16)
            vals = vals_s[:, pl.ds(off, rt)].astype(BF16)
            return k + 1, acc + lax.dot_general(onehot, vals, (((1,), (1,)), ((), ())),
                                                preferred_element_type=F32)

        k_end, acc = lax.while_loop(more, pick, (k_first, jnp.zeros((tr, PLAN_COLS), F32)))
        plan_ref[i] = acc
        digits = jnp.transpose(jnp.concatenate(
            [acc, jnp.zeros((tr, LANES - PLAN_COLS), F32)], axis=1))
        n = 3 * EXPERTS_PER_GROUP
        src_ref[i] = ((digits[n:n + 1, :] * 256.0 + digits[n + 1:n + 2, :]) * SUBLANES).astype(jnp.int32)
        return jnp.maximum(k_end - 1, k_first)

    lax.fori_loop(0, nt, tile_body, jnp.int32(0))


def _plan(route):
    t = route.shape[1]
    rt = ROUTE_TM
    tr = MOE_TR
    nt = t // tr + N_GROUPS
    idx = jnp.arange(rt)
    ut = (idx[:, None] <= idx[None, :]).astype(BF16)
    return pl.pallas_call(
        _plan_kernel,
        in_specs=[pl.BlockSpec(memory_space=pltpu.VMEM), pl.BlockSpec(memory_space=pltpu.VMEM)],
        out_specs=[pl.BlockSpec(memory_space=pltpu.VMEM), pl.BlockSpec(memory_space=pltpu.VMEM),
                   pl.BlockSpec(memory_space=pltpu.VMEM), pl.BlockSpec(memory_space=pltpu.SMEM)],
        out_shape=[jax.ShapeDtypeStruct((nt, tr, PLAN_COLS), F32),
                   jax.ShapeDtypeStruct((nt, 1, tr), jnp.int32),
                   jax.ShapeDtypeStruct((1, t), jnp.int32),
                   jax.ShapeDtypeStruct((nt,), jnp.int32)],
        scratch_shapes=[pltpu.VMEM((N_GROUPS, 1, t), F32), pltpu.VMEM((N_GROUPS, 1, t), F32),
                        pltpu.VMEM((PLAN_COLS, t), F32),
                        pltpu.SMEM(((t // rt + 1) * N_GROUPS,), jnp.int32)],
        compiler_params=pltpu.CompilerParams(vmem_limit_bytes=VMEM_LIMIT),
        name="moe_plan",
    )(route, ut)


def _start_row_gather(idx_ref, first, src_ref, buf, slot, sem):
    if not isinstance(slot, int):
        for s in range(buf.shape[0]):
            @pl.when(slot == s)
            def _():
                _start_row_gather(idx_ref, first, src_ref, buf, s, sem)
        return
    for r in range(buf.shape[1] // SUBLANES):
        row = pl.multiple_of(idx_ref[first + r], SUBLANES)
        pltpu.make_async_copy(src_ref.at[pl.ds(row, SUBLANES), :],
                              buf.at[slot, pl.ds(r * SUBLANES, SUBLANES), :], sem.at[slot]).start()


def _wait_row_gather(src_ref, buf, slot, sem):
    pltpu.make_async_copy(src_ref.at[pl.ds(0, buf.shape[1]), :], buf.at[slot], sem.at[slot]).wait()


def _ffn_kernel(tg_ref, src_ref, x_ref, plan_ref, w1_ref, w3_ref, w2_ref, ln_g_ref, ln_b_ref,
                o_ref, xbuf, sem, wb1, wb3, wb2, stage_in, stage_out, wsem):
    i = pl.program_id(0)
    last = pl.num_programs(0) - 1
    nslot = xbuf.shape[0]
    tr = xbuf.shape[1] // SUBLANES
    slot = i % nslot
    group = tg_ref[i]

    @pl.when(jnp.logical_or(i == 0, group != tg_ref[jnp.maximum(i - 1, 0)]))
    def _():
        chunks = [(w_hbm, wb, stage, e)
                  for w_hbm, wb, stage in ((w1_ref, wb1, stage_in), (w3_ref, wb3, stage_in),
                                           (w2_ref, wb2, stage_out))
                  for e in range(EXPERTS_PER_GROUP)]

        def copy(j):
            w_hbm, _, stage, e = chunks[j]
            return pltpu.make_async_copy(w_hbm.at[group, e], stage.at[j % 2], wsem.at[j % 2])

        copy(0).start()
        for j, (_, wb, stage, e) in enumerate(chunks):
            if j + 1 < len(chunks):
                copy(j + 1).start()
            copy(j).wait()
            wb[e] = stage[j % 2].astype(BF16)

    @pl.when(i == 0)
    def _():
        for ahead in range(nslot - 1):
            _start_row_gather(src_ref, jnp.minimum(ahead, last) * tr, x_ref, xbuf, ahead, sem)

    _wait_row_gather(x_ref, xbuf, slot, sem)
    x = _load_row_tiles(xbuf.at[slot], tr)
    xb = x.astype(BF16)
    plan = plan_ref[0]
    y = jnp.zeros(x.shape, F32)
    for e in range(EXPERTS_PER_GROUP):
        n = EXPERTS_PER_GROUP
        c = plan[:, e:e + 1] + plan[:, n + e:n + e + 1] + plan[:, 2 * n + e:2 * n + e + 1]
        h1 = jnp.dot(xb, wb1[e], preferred_element_type=F32)
        h3 = jnp.dot(xb, wb3[e], preferred_element_type=F32)
        hid = (h1 * jax.nn.sigmoid(h1)) * h3 * c
        y = y + jnp.dot(hid.astype(BF16), wb2[e], preferred_element_type=F32)
    _store_row_tiles(o_ref, _layer_norm(DN_ALPHA * x + y, ln_g_ref[...], ln_b_ref[...]))

    _start_row_gather(src_ref, jnp.minimum(i + nslot - 1, last) * tr, x_ref, xbuf,
                      (i + nslot - 1) % nslot, sem)

    @pl.when(i == last)
    def _():
        for ahead in range(1, nslot):
            _wait_row_gather(x_ref, xbuf, (i + ahead) % nslot, sem)


def _ffn(tile_group, src, x3, plan, w1, w3, w2, ln_g, ln_b):
    nt, tr, _ = plan.shape
    d, f = w1.shape[-2:]
    n = EXPERTS_PER_GROUP
    hbm = pl.BlockSpec(memory_space=pl.ANY)
    return pl.pallas_call(
        _ffn_kernel,
        grid_spec=pltpu.PrefetchScalarGridSpec(
            num_scalar_prefetch=2,
            grid=(nt,),
            in_specs=[hbm,
                      pl.BlockSpec((1, tr, PLAN_COLS), lambda i, tg, src: (i, 0, 0)),
                      hbm, hbm, hbm,
                      pl.BlockSpec((1, d), lambda i, tg, src: (0, 0)),
                      pl.BlockSpec((1, d), lambda i, tg, src: (0, 0))],
            out_specs=pl.BlockSpec((tr * SUBLANES, LANES), lambda i, tg, src: (i, 0)),
            scratch_shapes=[pltpu.VMEM((GATHER_SLOTS, tr * SUBLANES, LANES), F32),
                            pltpu.SemaphoreType.DMA((GATHER_SLOTS,)),
                            pltpu.VMEM((n, d, f), BF16), pltpu.VMEM((n, d, f), BF16),
                            pltpu.VMEM((n, f, d), BF16),
                            pltpu.VMEM((2, d, f), F32), pltpu.VMEM((2, f, d), F32),
                            pltpu.SemaphoreType.DMA((2,))]),
        out_shape=jax.ShapeDtypeStruct((nt * tr * SUBLANES, LANES), F32),
        compiler_params=pltpu.CompilerParams(
            dimension_semantics=("arbitrary",), vmem_limit_bytes=VMEM_LIMIT),
        name="moe_ffn",
    )(tile_group, src, x3, plan, w1, w3, w2, ln_g.reshape(1, -1), ln_b.reshape(1, -1))


def _unsort_kernel(dst_ref, y_ref, o_ref, buf, sem):
    i = pl.program_id(0)
    last = pl.num_programs(0) - 1
    nslot = buf.shape[0]
    tm = buf.shape[1] // SUBLANES
    slot = i % nslot

    @pl.when(i == 0)
    def _():
        for ahead in range(nslot - 1):
            _start_row_gather(dst_ref, jnp.minimum(ahead, last) * tm, y_ref, buf, ahead, sem)

    _wait_row_gather(y_ref, buf, slot, sem)
    o_ref[...] = _load_row_tiles(buf.at[slot], tm)
    _start_row_gather(dst_ref, jnp.minimum(i + nslot - 1, last) * tm, y_ref, buf,
                      (i + nslot - 1) % nslot, sem)

    @pl.when(i == last)
    def _():
        for ahead in range(1, nslot):
            _wait_row_gather(y_ref, buf, (i + ahead) % nslot, sem)


def _unsort(dst, y3, t):
    d = SUBLANES * LANES
    tm = UNSORT_TM
    return pl.pallas_call(
        _unsort_kernel,
        grid_spec=pltpu.PrefetchScalarGridSpec(
            num_scalar_prefetch=1,
            grid=(t // tm,),
            in_specs=[pl.BlockSpec(memory_space=pl.ANY)],
            out_specs=pl.BlockSpec((tm, d), lambda i, dst: (i, 0)),
            scratch_shapes=[pltpu.VMEM((GATHER_SLOTS, tm * SUBLANES, LANES), F32),
                            pltpu.SemaphoreType.DMA((GATHER_SLOTS,))]),
        out_shape=jax.ShapeDtypeStruct((t, d), F32),
        compiler_params=pltpu.CompilerParams(
            dimension_semantics=("arbitrary",), vmem_limit_bytes=VMEM_LIMIT),
        name="moe_unsort",
    )(dst, y3)


def _moe_layer(x3, w_group, b_group, w_router, b_router, w1, w3, w2, ln_g, ln_b):
    route = _router(x3, w_group, b_group, w_router, b_router)
    plan, src, dst, tile_group = _plan(route)
    y3 = _ffn(tile_group, src.reshape(-1), x3, plan, w1, w3, w2, ln_g, ln_b)
    return _unsort(dst.reshape(-1), y3, x3.shape[0] // SUBLANES)


def _qkv_kernel(x_ref, w_ref, o_ref):
    d = x_ref.shape[1]
    xb = x_ref[...].astype(BF16)
    for part in range(3):
        y = jnp.dot(xb, w_ref[:, part * d:(part + 1) * d], preferred_element_type=F32)
        if part == 0:
            y = y * (SB_HEAD_DIM ** -0.5 * LOG2_E)
        o_ref[:, part * d:(part + 1) * d] = y.astype(BF16)


def _qkv(xt, w_qkv):
    t, d = xt.shape
    tm = PROJ_TM
    return pl.pallas_call(
        _qkv_kernel,
        grid=(t // tm,),
        in_specs=[pl.BlockSpec((tm, d), lambda i: (i, 0)),
                  pl.BlockSpec((d, 3 * d), lambda i: (0, 0))],
        out_specs=pl.BlockSpec((tm, 3 * d), lambda i: (i, 0)),
        out_shape=jax.ShapeDtypeStruct((t, 3 * d), BF16),
        compiler_params=pltpu.CompilerParams(
            dimension_semantics=("arbitrary",), vmem_limit_bytes=VMEM_LIMIT),
        name="qkv_proj",
    )(xt, w_qkv.astype(BF16))


def _softplus2(z2):
    return jnp.maximum(z2, 0.0) + jnp.log2(1.0 + jnp.exp2(-jnp.abs(z2)))


def _attention_kernel(q_ref, k_ref, v_ref, o_ref):
    t = ATT_T
    nq = q_ref.shape[1] // t
    nt_dims = (((1,), (1,)), ((), ()))
    lane = lax.broadcasted_iota(jnp.int32, (t, LANES), 1)
    first_head = lane < SB_HEAD_DIM
    row = lax.broadcasted_iota(jnp.int32, (t, t), 0)
    col = lax.broadcasted_iota(jnp.int32, (t, t), 1)
    tri = jnp.where(row >= col, 1.0, 0.0).astype(BF16)
    strict = col < row

    def tile(i, _):
        q_start = pl.multiple_of(i * t, t)
        p_start = pl.multiple_of(jnp.maximum(i - 1, 0) * t, t)
        has_prev = i > 0
        q = q_ref[0, pl.ds(q_start, t), :]
        q_heads = (jnp.where(first_head, q, jnp.zeros_like(q)),
                   jnp.where(first_head, jnp.zeros_like(q), q))
        keys = {"d": k_ref[0, pl.ds(q_start, t), :], "p": k_ref[0, pl.ds(p_start, t), :]}
        vals = {"d": v_ref[0, pl.ds(q_start, t), :], "p": v_ref[0, pl.ds(p_start, t), :]}

        chains = [(h, b) for b in ("d", "p") for h in range(2)]
        zs = {c: lax.dot_general(q_heads[c[0]], keys[c[1]], nt_dims, preferred_element_type=F32)
              for c in chains}
        sps, cs, atts, outs = {}, {}, {}, {}

        def keep(c, x):
            return jnp.where(strict, x, 0.0) if c[1] == "d" else jnp.where(has_prev, x, 0.0)

        def sp_of(c):
            return keep(c, _softplus2(zs[c])).astype(BF16)

        def att_of(c):
            e = zs[c] - cs[c]
            if c[1] == "p":
                e = e - cs[(c[0], "d")][:, 0:1]
            return keep(c, jnp.exp2(e)).astype(BF16)

        sps[chains[0]] = sp_of(chains[0])
        for n, c in enumerate(chains):
            cs[c] = jnp.dot(sps[c], tri, preferred_element_type=F32)
            if n + 1 < len(chains):
                sps[chains[n + 1]] = sp_of(chains[n + 1])
        r0, r1 = [cs[(h, "d")][:, 0:1] + cs[(h, "p")][:, 0:1] for h in range(2)]
        r_min = jnp.min(jnp.minimum(r0, r1))
        atts[chains[0]] = att_of(chains[0])
        for n, c in enumerate(chains):
            outs[c] = jnp.dot(atts[c], vals[c[1]], preferred_element_type=F32)
            if n + 1 < len(chains):
                atts[chains[n + 1]] = att_of(chains[n + 1])
        acc = jnp.where(first_head, outs[(0, "d")] + outs[(0, "p")], outs[(1, "d")] + outs[(1, "p")])

        def block(kb, acc, r0, r1):
            start = pl.multiple_of(kb * t, t)
            k = k_ref[0, pl.ds(start, t), :]
            v = v_ref[0, pl.ds(start, t), :]
            z = [lax.dot_general(qh, k, nt_dims, preferred_element_type=F32) for qh in q_heads]
            sp0 = _softplus2(z[0]).astype(BF16)
            c0 = jnp.dot(sp0, tri, preferred_element_type=F32)
            sp1 = _softplus2(z[1]).astype(BF16)
            c1 = jnp.dot(sp1, tri, preferred_element_type=F32)
            att0 = jnp.exp2(z[0] - c0 - r0).astype(BF16)
            o0 = jnp.dot(att0, v, preferred_element_type=F32)
            att1 = jnp.exp2(z[1] - c1 - r1).astype(BF16)
            o1 = jnp.dot(att1, v, preferred_element_type=F32)
            r0n = r0 + c0[:, 0:1]
            r1n = r1 + c1[:, 0:1]
            return (acc + jnp.where(first_head, o0, o1), r0n, r1n,
                    jnp.min(jnp.minimum(r0n, r1n)))

        def cond(c):
            return jnp.logical_and(c[0] >= 0, c[4] < ATT_DEAD)

        def body(c):
            kb, acc, r0, r1, _ = c
            acc, r0, r1, r_min = block(kb, acc, r0, r1)
            return kb - 1, acc, r0, r1, r_min

        carry = lax.while_loop(cond, body, (i - 2, acc, r0, r1, r_min))
        o_ref[0, pl.ds(q_start, t), :] = carry[1].astype(o_ref.dtype)
        return 0

    lax.fori_loop(0, nq, tile, 0)


def _attention(qkv, bsz, seq):
    d = qkv.shape[-1] // 3
    qkv = qkv.reshape(bsz, seq, 3 * d)
    t = ATT_T
    npair = d // LANES
    return pl.pallas_call(
        _attention_kernel,
        grid=(bsz, npair),
        in_specs=[pl.BlockSpec((1, seq, LANES), lambda b, p: (b, 0, p)),
                  pl.BlockSpec((1, seq, LANES), lambda b, p: (b, 0, npair + p)),
                  pl.BlockSpec((1, seq, LANES), lambda b, p: (b, 0, 2 * npair + p))],
        out_specs=pl.BlockSpec((1, seq, LANES), lambda b, p: (b, 0, p)),
        out_shape=jax.ShapeDtypeStruct((bsz, seq, d), BF16),
        compiler_params=pltpu.CompilerParams(
            dimension_semantics=("arbitrary", "arbitrary"), vmem_limit_bytes=VMEM_LIMIT),
        name="sb_attention",
    )(qkv, qkv, qkv)


def _proj_ln_kernel(y_ref, x_ref, w_ref, ln_g_ref, ln_b_ref, o_ref):
    mix = jnp.dot(y_ref[...], w_ref[...], preferred_element_type=F32)
    _store_row_tiles(o_ref, _layer_norm(DN_ALPHA * x_ref[...] + mix, ln_g_ref[...], ln_b_ref[...]))


def _proj_ln(y, xt, w, ln_g, ln_b):
    t, d = xt.shape
    tm = PROJ_TM
    return pl.pallas_call(
        _proj_ln_kernel,
        grid=(t // tm,),
        in_specs=[pl.BlockSpec((tm, d), lambda i: (i, 0)),
                  pl.BlockSpec((tm, d), lambda i: (i, 0)),
                  pl.BlockSpec((d, d), lambda i: (0, 0)),
                  pl.BlockSpec((1, d), lambda i: (0, 0)),
                  pl.BlockSpec((1, d), lambda i: (0, 0))],
        out_specs=pl.BlockSpec((tm * SUBLANES, LANES), lambda i: (i, 0)),
        out_shape=jax.ShapeDtypeStruct((t * SUBLANES, LANES), F32),
        compiler_params=pltpu.CompilerParams(
            dimension_semantics=("arbitrary",), vmem_limit_bytes=VMEM_LIMIT),
        name="out_proj_ln",
    )(y, xt, w.astype(BF16), ln_g.reshape(1, -1), ln_b.reshape(1, -1))


def _sb_mixer(x, w_qkv, w_out, ln_g, ln_b):
    bsz, seq, d = x.shape
    xt = x.reshape(-1, d)
    o = _attention(_qkv(xt, w_qkv), bsz, seq)
    return _proj_ln(o.reshape(-1, d), xt, w_out, ln_g, ln_b)


def kernel(x, even_w_in, even_conv_w, even_sgu_ln_g, even_sgu_ln_b, even_sgu_w_s, even_sgu_b_s, even_w_out, odd_w_qkv, odd_w_out, mix_ln_g, mix_ln_b, moe_w_group, moe_b_group, moe_w_router, moe_b_router, moe_w1, moe_w3, moe_w2, ffn_ln_g, ffn_ln_b):
    shape = x.shape
    for layer in range(DEPTH):
        i = layer // 2
        if layer % 2 == 0:
            x3 = _mixer0(x, even_w_in[i], even_conv_w[i], even_sgu_ln_g[i], even_sgu_ln_b[i],
                         even_sgu_w_s[i], even_sgu_b_s[i], even_w_out[i], mix_ln_g[layer], mix_ln_b[layer])
        else:
            x3 = _sb_mixer(x, odd_w_qkv[i], odd_w_out[i], mix_ln_g[layer], mix_ln_b[layer])
        x = _moe_layer(x3, moe_w_group[layer], moe_b_group[layer], moe_w_router[layer],
                       moe_b_router[layer], moe_w1[layer], moe_w3[layer], moe_w2[layer],
                       ffn_ln_g[layer], ffn_ln_b[layer]).reshape(shape)
    return x
```

```python
import functools

import jax
import jax.numpy as jnp
from jax import lax
from jax.experimental import pallas as pl
from jax.experimental.pallas import tpu as pltpu

F32 = jnp.float32
BF16 = jnp.bfloat16

DEPTH = 2
DN_ALPHA = (2 * DEPTH) ** 0.25
LN_EPS = 1e-5

CONV_WIDTH = 512
CONV_K = 3
SGU_WIDTH = 512
SGU_HEADS = 8
SGU_HEAD_DIM = 64
SGU_BLOCK = 128
CHUNK = 64
SB_HEADS = 16
SB_HEAD_DIM = 64
N_GROUPS = 4
EXPERTS_PER_GROUP = 4
N_EXPERTS = N_GROUPS * EXPERTS_PER_GROUP

LANES = 128
SUBLANES = 8
VMEM_LIMIT = 56 * 1024 * 1024

MIX_TM = 512
PROJ_TM = 1024
ROUTE_TM = 1024
ROUTE_ROWS = 8
PLAN_COLS = 16
MOE_TR = 256
GATHER_SLOTS = 3
UNSORT_TM = 512
ATT_T = 256
LOG2_E = 1.4426950408889634
ATT_DEAD = 150.0


def _layer_norm(r, g, b):
    mu = jnp.mean(r, axis=-1, keepdims=True)
    c = r - mu
    var = jnp.mean(c * c, axis=-1, keepdims=True)
    return c * lax.rsqrt(var + LN_EPS) * g + b


def _gelu_tanh(x):
    return 0.5 * x * (1.0 + jnp.tanh(0.7978845608028654 * (x + 0.044715 * (x * x * x))))


def _store_row_tiles(ref, val):
    n, d = val.shape
    assert d == SUBLANES * LANES
    for c in range(SUBLANES):
        ref[pl.ds(c, n, stride=SUBLANES), :] = val[:, c * LANES:(c + 1) * LANES]


def _load_row_tiles(ref, n):
    return jnp.concatenate([ref[pl.ds(c, n, stride=SUBLANES), :] for c in range(SUBLANES)], axis=1)


def _mixer0_kernel(x_ref, w_in_ref, conv_w_ref, sgu_g_ref, sgu_b_ref, ws_ref, bs_ref, w_out_ref,
                   ln_g_ref, ln_b_ref, o_ref, tail_ref, y_ref):
    tm = x_ref.shape[1]

    @pl.when(pl.program_id(1) == 0)
    def _():
        tail_ref[...] = jnp.zeros_like(tail_ref)

    x = x_ref[0]
    xb = x.astype(BF16)
    cw = CONV_WIDTH
    bch = jnp.dot(xb, w_in_ref[:, :3 * cw], preferred_element_type=F32)
    b_gate, c_gate, h = bch[:, :cw], bch[:, cw:2 * cw], bch[:, 2 * cw:]
    u = c_gate * h
    cat = jnp.concatenate([tail_ref[...], u], axis=0)
    tail_ref[...] = u[tm - 8:, :]
    u1 = cat[7:7 + tm, :]
    u2 = cat[6:6 + tm, :]
    conv = conv_w_ref[0:1, :] * u2 + conv_w_ref[1:2, :] * u1 + conv_w_ref[2:3, :] * u
    y_ref[:, :cw] = (b_gate * conv).astype(BF16)

    z = _gelu_tanh(jnp.dot(xb, w_in_ref[:, 3 * cw:], preferred_element_type=F32))
    z_u, z_v = z[:, :SGU_WIDTH], z[:, SGU_WIDTH:]
    v = _layer_norm(z_v, sgu_g_ref[...], sgu_b_ref[...])
    lane = lax.broadcasted_iota(jnp.int32, (tm, LANES), 1)
    first_head = lane < SGU_HEAD_DIM
    nblk = tm // SGU_BLOCK
    for j in range(SGU_HEADS // 2):
        vp = v[:, j * LANES:(j + 1) * LANES]
        v_a = jnp.where(first_head, vp, 0.0).astype(BF16)
        v_b = jnp.where(first_head, 0.0, vp).astype(BF16)
        rhs = jnp.concatenate(
            [jnp.concatenate([v_a[n * SGU_BLOCK:(n + 1) * SGU_BLOCK],
                              v_b[n * SGU_BLOCK:(n + 1) * SGU_BLOCK]], axis=0)
             for n in range(nblk)], axis=1)
        mixed = jnp.dot(ws_ref[j], rhs, preferred_element_type=F32)
        bias = bs_ref[:, j * LANES:(j + 1) * LANES]
        for n in range(nblk):
            rows = slice(n * SGU_BLOCK, (n + 1) * SGU_BLOCK)
            gate = z_u[rows, j * LANES:(j + 1) * LANES]
            y_ref[rows, cw + j * LANES:cw + (j + 1) * LANES] = (
                gate * (mixed[:, n * LANES:(n + 1) * LANES] + bias)).astype(BF16)

    mix = jnp.dot(y_ref[...], w_out_ref[...], preferred_element_type=F32)
    _store_row_tiles(o_ref, _layer_norm(DN_ALPHA * x + mix, ln_g_ref[...], ln_b_ref[...]))


def _mixer0(x, w_in, conv_w, sgu_g, sgu_b, w_s, b_s, w_out, ln_g, ln_b):
    bsz, seq, d = x.shape
    tm = MIX_TM
    pos = jnp.arange(SGU_BLOCK)
    chunk_causal = (pos[None, :] // CHUNK) <= (pos[:, None] // CHUNK)
    w_masked = jnp.where(chunk_causal[None], w_s, 0.0).astype(BF16)
    ws_pairs = jnp.concatenate([w_masked[0::2], w_masked[1::2]], axis=2)
    bias = jnp.repeat(jnp.transpose(b_s), SGU_HEAD_DIM, axis=1)
    full = lambda shape: pl.BlockSpec(shape, lambda b, s: (0,) * len(shape))
    return pl.pallas_call(
        _mixer0_kernel,
        grid=(bsz, seq // tm),
        in_specs=[
            pl.BlockSpec((1, tm, d), lambda b, s: (b, s, 0)),
            full(w_in.shape), full(conv_w.shape), full((1, SGU_WIDTH)), full((1, SGU_WIDTH)),
            full(ws_pairs.shape), full(bias.shape), full(w_out.shape),
            full((1, d)), full((1, d)),
        ],
        out_specs=pl.BlockSpec((tm * SUBLANES, LANES), lambda b, s: (b * (seq // tm) + s, 0)),
        out_shape=jax.ShapeDtypeStruct((bsz * seq * SUBLANES, LANES), F32),
        scratch_shapes=[pltpu.VMEM((8, CONV_WIDTH), F32), pltpu.VMEM((tm, d), BF16)],
        compiler_params=pltpu.CompilerParams(
            dimension_semantics=("arbitrary", "arbitrary"), vmem_limit_bytes=VMEM_LIMIT),
        name="mixer0",
    )(x, w_in.astype(BF16), conv_w, sgu_g.reshape(1, -1), sgu_b.reshape(1, -1), ws_pairs, bias,
      w_out.astype(BF16), ln_g.reshape(1, -1), ln_b.reshape(1, -1))


def _router_kernel(x_ref, w_ref, b_ref, route_ref):
    x = _load_row_tiles(x_ref, x_ref.shape[0] // SUBLANES)
    xh = x.astype(BF16)
    xl = (x - xh.astype(F32)).astype(BF16)
    w = w_ref[...]
    wh = w.astype(BF16)
    wl = (w - wh.astype(F32)).astype(BF16)
    logits_t = (jnp.dot(xh, wh, preferred_element_type=F32)
                + jnp.dot(xl, wh, preferred_element_type=F32)
                + jnp.dot(xh, wl, preferred_element_type=F32))
    logits = jnp.transpose(logits_t)[:b_ref.shape[0], :] + b_ref[...]
    gl = [logits[g:g + 1, :] for g in range(N_GROUPS)]
    g_max = functools.reduce(jnp.maximum, gl)
    g_top = 1.0 / functools.reduce(lambda a, b: a + b, [jnp.exp(l - g_max) for l in gl])
    g_idx = jnp.full_like(g_max, N_GROUPS).astype(jnp.int32)
    for g in reversed(range(N_GROUPS)):
        g_idx = jnp.where(gl[g] == g_max, g, g_idx)
    el = []
    for e in range(EXPERTS_PER_GROUP):
        acc = jnp.zeros_like(g_max)
        for g in range(N_GROUPS):
            r = N_GROUPS + g * EXPERTS_PER_GROUP + e
            acc = jnp.where(g_idx == g, logits[r:r + 1, :], acc)
        el.append(acc)
    m1 = functools.reduce(jnp.maximum, el)
    i1 = jnp.full_like(g_idx, EXPERTS_PER_GROUP)
    for e in reversed(range(EXPERTS_PER_GROUP)):
        i1 = jnp.where(el[e] == m1, e, i1)
    rest = [jnp.where(i1 == e, -jnp.inf, el[e]) for e in range(EXPERTS_PER_GROUP)]
    m2 = functools.reduce(jnp.maximum, rest)
    i2 = jnp.full_like(g_idx, EXPERTS_PER_GROUP)
    for e in reversed(range(EXPERTS_PER_GROUP)):
        i2 = jnp.where(rest[e] == m2, e, i2)
    p2 = jnp.exp(m2 - m1)
    w1 = g_top / (1.0 + p2)
    w2 = g_top * p2 / (1.0 + p2)
    for g in range(N_GROUPS):
        route_ref[g:g + 1, :] = jnp.where(g_idx == g, 1.0, 0.0)
    for e in range(EXPERTS_PER_GROUP):
        route_ref[N_GROUPS + e:N_GROUPS + e + 1, :] = jnp.where(
            i1 == e, w1, jnp.where(i2 == e, w2, 0.0))


def _router(x3, w_group, b_group, w_router, b_router):
    d = SUBLANES * LANES
    t = x3.shape[0] // SUBLANES
    tm = ROUTE_TM
    rows = 32
    w_all = jnp.concatenate(
        [jnp.transpose(w_group), jnp.transpose(w_router, (0, 2, 1)).reshape(N_EXPERTS, d)], axis=0)
    w_t = jnp.transpose(jnp.pad(w_all, ((0, LANES - w_all.shape[0]), (0, 0))))
    b_all = jnp.pad(jnp.concatenate([b_group, b_router.reshape(-1)]), (0, rows - N_GROUPS - N_EXPERTS))
    return pl.pallas_call(
        _router_kernel,
        grid=(t // tm,),
        in_specs=[pl.BlockSpec((tm * SUBLANES, LANES), lambda i: (i, 0)),
                  pl.BlockSpec((d, LANES), lambda i: (0, 0)),
                  pl.BlockSpec((rows, 1), lambda i: (0, 0))],
        out_specs=pl.BlockSpec((ROUTE_ROWS, tm), lambda i: (0, i)),
        out_shape=jax.ShapeDtypeStruct((ROUTE_ROWS, t), F32),
        compiler_params=pltpu.CompilerParams(
            dimension_semantics=("arbitrary",), vmem_limit_bytes=VMEM_LIMIT),
        name="router",
    )(x3, w_t, b_all.reshape(rows, 1))


def _plan_kernel(route_ref, ut_ref, plan_ref, src_ref, dst_ref, tg_ref, cum_s, m_s, vals_s, cin_s):
    t = route_ref.shape[1]
    rt = ut_ref.shape[0]
    nk = t // rt
    nt, tr, _ = plan_ref.shape
    lane = lax.broadcasted_iota(jnp.int32, (1, rt), 1)

    def cum_body(k, carry):
        off = pl.multiple_of(k * rt, rt)
        route = route_ref[:, pl.ds(off, rt)]
        cum = jnp.dot(route.astype(BF16), ut_ref[...], preferred_element_type=F32) + carry
        comb = route[N_GROUPS:, :]
        b0 = comb.astype(BF16).astype(F32)
        b1 = (comb - b0).astype(BF16).astype(F32)
        b2 = (comb - b0 - b1).astype(BF16).astype(F32)
        tok = off + lane
        pieces = ([b0[e:e + 1] for e in range(EXPERTS_PER_GROUP)]
                  + [b1[e:e + 1] for e in range(EXPERTS_PER_GROUP)]
                  + [b2[e:e + 1] for e in range(EXPERTS_PER_GROUP)]
                  + [(tok >> 8).astype(F32), (tok & 255).astype(F32)])
        for r, piece in enumerate(pieces):
            vals_s[r:r + 1, pl.ds(off, rt)] = piece
        for r in range(len(pieces), PLAN_COLS):
            vals_s[r:r + 1, pl.ds(off, rt)] = jnp.zeros((1, rt), F32)
        for g in range(N_GROUPS):
            cum_s[g, :, pl.ds(off, rt)] = cum[g:g + 1, :]
            m_s[g, :, pl.ds(off, rt)] = route[g:g + 1, :]
            cin_s[k * N_GROUPS + g] = carry[g, 0].astype(jnp.int32)
        return cum[:, rt - 1:rt]

    total = lax.fori_loop(0, nk, cum_body, jnp.zeros((ROUTE_ROWS, 1), F32))
    tile_base = [jnp.int32(0)]
    for g in range(N_GROUPS):
        cnt = total[g, 0].astype(jnp.int32)
        cin_s[nk * N_GROUPS + g] = cnt
        tile_base.append(tile_base[-1] + (cnt + tr - 1) // tr)

    def dst_body(k, _):
        off = pl.multiple_of(k * rt, rt)
        pos = jnp.zeros((1, rt), F32)
        for g in range(N_GROUPS):
            pos = pos + m_s[g, :, pl.ds(off, rt)] * (
                cum_s[g, :, pl.ds(off, rt)] + (tile_base[g] * tr - 1).astype(F32))
        dst_ref[:, pl.ds(off, rt)] = (pos * SUBLANES).astype(jnp.int32)
        return 0

    lax.fori_loop(0, nk, dst_body, 0)

    row_iota = lax.broadcasted_iota(jnp.int32, (tr, rt), 0).astype(F32)

    def tile_body(i, k_prev):
        g = ((i >= tile_base[1]).astype(jnp.int32) + (i >= tile_base[2]).astype(jnp.int32)
             + (i >= tile_base[3]).astype(jnp.int32))
        base = jnp.where(g == 0, tile_base[0],
                         jnp.where(g == 1, tile_base[1], jnp.where(g == 2, tile_base[2], tile_base[3])))
        r0 = (i - base) * tr
        tg_ref[i] = g
        target = row_iota + (r0 + 1).astype(F32)

        k_first = lax.while_loop(
            lambda k: jnp.logical_and(k < nk, cin_s[jnp.minimum(k + 1, nk) * N_GROUPS + g] <= r0),
            lambda k: k + 1, jnp.where(i == base, 0, k_prev))

        def more(c):
            return jnp.logical_and(c[0] < nk, cin_s[c[0] * N_GROUPS + g] < r0 + tr)

        def pick(c):
            k, acc = c
            off = pl.multiple_of(k * rt, rt)
            onehot = jnp.where(cum_s[g, :, pl.ds(off, rt)] == target,
                               m_s[g, :, pl.ds(off, rt)], 0.0).astype(BF16)
            vals = vals_s[:, pl.ds(off, rt)].astype(BF16)
            return k + 1, acc + lax.dot_general(onehot, vals, (((1,), (1,)), ((), ())),
                                                preferred_element_type=F32)

        k_end, acc = lax.while_loop(more, pick, (k_first, jnp.zeros((tr, PLAN_COLS), F32)))
        plan_ref[i] = acc
        digits = jnp.transpose(jnp.concatenate(
            [acc, jnp.zeros((tr, LANES - PLAN_COLS), F32)], axis=1))
        n = 3 * EXPERTS_PER_GROUP
        src_ref[i] = ((digits[n:n + 1, :] * 256.0 + digits[n + 1:n + 2, :]) * SUBLANES).astype(jnp.int32)
        return jnp.maximum(k_end - 1, k_first)

    lax.fori_loop(0, nt, tile_body, jnp.int32(0))


def _plan(route):
    t = route.shape[1]
    rt = ROUTE_TM
    tr = MOE_TR
    nt = t // tr + N_GROUPS
    idx = jnp.arange(rt)
    ut = (idx[:, None] <= idx[None, :]).astype(BF16)
    return pl.pallas_call(
        _plan_kernel,
        in_specs=[pl.BlockSpec(memory_space=pltpu.VMEM), pl.BlockSpec(memory_space=pltpu.VMEM)],
        out_specs=[pl.BlockSpec(memory_space=pltpu.VMEM), pl.BlockSpec(memory_space=pltpu.VMEM),
                   pl.BlockSpec(memory_space=pltpu.VMEM), pl.BlockSpec(memory_space=pltpu.SMEM)],
        out_shape=[jax.ShapeDtypeStruct((nt, tr, PLAN_COLS), F32),
                   jax.ShapeDtypeStruct((nt, 1, tr), jnp.int32),
                   jax.ShapeDtypeStruct((1, t), jnp.int32),
                   jax.ShapeDtypeStruct((nt,), jnp.int32)],
        scratch_shapes=[pltpu.VMEM((N_GROUPS, 1, t), F32), pltpu.VMEM((N_GROUPS, 1, t), F32),
                        pltpu.VMEM((PLAN_COLS, t), F32),
                        pltpu.SMEM(((t // rt + 1) * N_GROUPS,), jnp.int32)],
        compiler_params=pltpu.CompilerParams(vmem_limit_bytes=VMEM_LIMIT),
        name="moe_plan",
    )(route, ut)


def _start_row_gather(idx_ref, first, src_ref, buf, slot, sem):
    if not isinstance(slot, int):
        for s in range(buf.shape[0]):
            @pl.when(slot == s)
            def _():
                _start_row_gather(idx_ref, first, src_ref, buf, s, sem)
        return
    for r in range(buf.shape[1] // SUBLANES):
        row = pl.multiple_of(idx_ref[first + r], SUBLANES)
        pltpu.make_async_copy(src_ref.at[pl.ds(row, SUBLANES), :],
                              buf.at[slot, pl.ds(r * SUBLANES, SUBLANES), :], sem.at[slot]).start()


def _wait_row_gather(src_ref, buf, slot, sem):
    pltpu.make_async_copy(src_ref.at[pl.ds(0, buf.shape[1]), :], buf.at[slot], sem.at[slot]).wait()


def _ffn_kernel(layer, tg_ref, src_ref, x_ref, plan_ref, w1_ref, w3_ref, w2_ref, ln_g_ref, ln_b_ref,
                o_ref, xbuf, sem, wb1, wb3, wb2, stage_in, stage_out, wsem):
    i = pl.program_id(0)
    last = pl.num_programs(0) - 1
    nslot = xbuf.shape[0]
    tr = xbuf.shape[1] // SUBLANES
    slot = i % nslot
    group = tg_ref[i]

    @pl.when(jnp.logical_or(i == 0, group != tg_ref[jnp.maximum(i - 1, 0)]))
    def _():
        chunks = [(w_hbm, wb, stage, e)
                  for w_hbm, wb, stage in ((w1_ref, wb1, stage_in), (w3_ref, wb3, stage_in),
                                           (w2_ref, wb2, stage_out))
                  for e in range(EXPERTS_PER_GROUP)]

        def copy(j):
            w_hbm, _, stage, e = chunks[j]
            return pltpu.make_async_copy(w_hbm.at[layer, group, e], stage.at[j % 2], wsem.at[j % 2])

        copy(0).start()
        for j, (_, wb, stage, e) in enumerate(chunks):
            if j + 1 < len(chunks):
                copy(j + 1).start()
            copy(j).wait()
            wb[e] = stage[j % 2].astype(BF16)

    @pl.when(i == 0)
    def _():
        for ahead in range(nslot - 1):
            _start_row_gather(src_ref, jnp.minimum(ahead, last) * tr, x_ref, xbuf, ahead, sem)

    _wait_row_gather(x_ref, xbuf, slot, sem)
    x = _load_row_tiles(xbuf.at[slot], tr)
    xb = x.astype(BF16)
    plan = plan_ref[0]
    y = jnp.zeros(x.shape, F32)
    for e in range(EXPERTS_PER_GROUP):
        n = EXPERTS_PER_GROUP
        c = plan[:, e:e + 1] + plan[:, n + e:n + e + 1] + plan[:, 2 * n + e:2 * n + e + 1]
        h1 = jnp.dot(xb, wb1[e], preferred_element_type=F32)
        h3 = jnp.dot(xb, wb3[e], preferred_element_type=F32)
        hid = (h1 * jax.nn.sigmoid(h1)) * h3 * c
        y = y + jnp.dot(hid.astype(BF16), wb2[e], preferred_element_type=F32)
    _store_row_tiles(o_ref, _layer_norm(DN_ALPHA * x + y, ln_g_ref[...], ln_b_ref[...]))

    _start_row_gather(src_ref, jnp.minimum(i + nslot - 1, last) * tr, x_ref, xbuf,
                      (i + nslot - 1) % nslot, sem)

    @pl.when(i == last)
    def _():
        for ahead in range(1, nslot):
            _wait_row_gather(x_ref, xbuf, (i + ahead) % nslot, sem)


def _ffn(tile_group, src, x3, plan, w1, w3, w2, layer, ln_g, ln_b):
    nt, tr, _ = plan.shape
    d, f = w1.shape[-2:]
    n = EXPERTS_PER_GROUP
    hbm = pl.BlockSpec(memory_space=pl.ANY)
    return pl.pallas_call(
        functools.partial(_ffn_kernel, layer),
        grid_spec=pltpu.PrefetchScalarGridSpec(
            num_scalar_prefetch=2,
            grid=(nt,),
            in_specs=[hbm,
                      pl.BlockSpec((1, tr, PLAN_COLS), lambda i, tg, src: (i, 0, 0)),
                      hbm, hbm, hbm,
                      pl.BlockSpec((1, d), lambda i, tg, src: (0, 0)),
                      pl.BlockSpec((1, d), lambda i, tg, src: (0, 0))],
            out_specs=pl.BlockSpec((tr * SUBLANES, LANES), lambda i, tg, src: (i, 0)),
            scratch_shapes=[pltpu.VMEM((GATHER_SLOTS, tr * SUBLANES, LANES), F32),
                            pltpu.SemaphoreType.DMA((GATHER_SLOTS,)),
                            pltpu.VMEM((n, d, f), BF16), pltpu.VMEM((n, d, f), BF16),
                            pltpu.VMEM((n, f, d), BF16),
                            pltpu.VMEM((2, d, f), F32), pltpu.VMEM((2, f, d), F32),
                            pltpu.SemaphoreType.DMA((2,))]),
        out_shape=jax.ShapeDtypeStruct((nt * tr * SUBLANES, LANES), F32),
        compiler_params=pltpu.CompilerParams(
            dimension_semantics=("arbitrary",), vmem_limit_bytes=VMEM_LIMIT),
        name="moe_ffn",
    )(tile_group, src, x3, plan, w1, w3, w2, ln_g.reshape(1, -1), ln_b.reshape(1, -1))


def _unsort_kernel(dst_ref, y_ref, o_ref, buf, sem):
    i = pl.program_id(0)
    last = pl.num_programs(0) - 1
    nslot = buf.shape[0]
    tm = buf.shape[1] // SUBLANES
    slot = i % nslot

    @pl.when(i == 0)
    def _():
        for ahead in range(nslot - 1):
            _start_row_gather(dst_ref, jnp.minimum(ahead, last) * tm, y_ref, buf, ahead, sem)

    _wait_row_gather(y_ref, buf, slot, sem)
    o_ref[...] = _load_row_tiles(buf.at[slot], tm)
    _start_row_gather(dst_ref, jnp.minimum(i + nslot - 1, last) * tm, y_ref, buf,
                      (i + nslot - 1) % nslot, sem)

    @pl.when(i == last)
    def _():
        for ahead in range(1, nslot):
            _wait_row_gather(y_ref, buf, (i + ahead) % nslot, sem)


def _unsort(dst, y3, t):
    d = SUBLANES * LANES
    tm = UNSORT_TM
    return pl.pallas_call(
        _unsort_kernel,
        grid_spec=pltpu.PrefetchScalarGridSpec(
            num_scalar_prefetch=1,
            grid=(t // tm,),
            in_specs=[pl.BlockSpec(memory_space=pl.ANY)],
            out_specs=pl.BlockSpec((tm, d), lambda i, dst: (i, 0)),
            scratch_shapes=[pltpu.VMEM((GATHER_SLOTS, tm * SUBLANES, LANES), F32),
                            pltpu.SemaphoreType.DMA((GATHER_SLOTS,))]),
        out_shape=jax.ShapeDtypeStruct((t, d), F32),
        compiler_params=pltpu.CompilerParams(
            dimension_semantics=("arbitrary",), vmem_limit_bytes=VMEM_LIMIT),
        name="moe_unsort",
    )(dst, y3)


def _moe_layer(x3, w_group, b_group, w_router, b_router, w1, w3, w2, layer, ln_g, ln_b):
    route = _router(x3, w_group, b_group, w_router, b_router)
    plan, src, dst, tile_group = _plan(route)
    y3 = _ffn(tile_group, src.reshape(-1), x3, plan, w1, w3, w2, layer, ln_g, ln_b)
    return _unsort(dst.reshape(-1), y3, x3.shape[0] // SUBLANES)


def _qkv_kernel(x_ref, w_ref, o_ref):
    d = x_ref.shape[1]
    xb = x_ref[...].astype(BF16)
    for part in range(3):
        y = jnp.dot(xb, w_ref[:, part * d:(part + 1) * d], preferred_element_type=F32)
        if part == 0:
            y = y * (SB_HEAD_DIM ** -0.5 * LOG2_E)
        o_ref[:, part * d:(part + 1) * d] = y.astype(BF16)


def _qkv(xt, w_qkv):
    t, d = xt.shape
    tm = PROJ_TM
    return pl.pallas_call(
        _qkv_kernel,
        grid=(t // tm,),
        in_specs=[pl.BlockSpec((tm, d), lambda i: (i, 0)),
                  pl.BlockSpec((d, 3 * d), lambda i: (0, 0))],
        out_specs=pl.BlockSpec((tm, 3 * d), lambda i: (i, 0)),
        out_shape=jax.ShapeDtypeStruct((t, 3 * d), BF16),
        compiler_params=pltpu.CompilerParams(
            dimension_semantics=("arbitrary",), vmem_limit_bytes=VMEM_LIMIT),
        name="qkv_proj",
    )(xt, w_qkv.astype(BF16))


def _softplus2(z2):
    return jnp.maximum(z2, 0.0) + jnp.log2(1.0 + jnp.exp2(-jnp.abs(z2)))


def _attention_kernel(q_ref, k_ref, v_ref, o_ref):
    t = ATT_T
    nq = q_ref.shape[1] // t
    nt_dims = (((1,), (1,)), ((), ()))
    lane = lax.broadcasted_iota(jnp.int32, (t, LANES), 1)
    first_head = lane < SB_HEAD_DIM
    row = lax.broadcasted_iota(jnp.int32, (t, t), 0)
    col = lax.broadcasted_iota(jnp.int32, (t, t), 1)
    tri = jnp.where(row >= col, 1.0, 0.0).astype(BF16)
    strict = col < row

    def tile(i, _):
        q_start = pl.multiple_of(i * t, t)
        p_start = pl.multiple_of(jnp.maximum(i - 1, 0) * t, t)
        has_prev = i > 0
        q = q_ref[0, pl.ds(q_start, t), :]
        q_heads = (jnp.where(first_head, q, jnp.zeros_like(q)),
                   jnp.where(first_head, jnp.zeros_like(q), q))
        keys = {"d": k_ref[0, pl.ds(q_start, t), :], "p": k_ref[0, pl.ds(p_start, t), :]}
        vals = {"d": v_ref[0, pl.ds(q_start, t), :], "p": v_ref[0, pl.ds(p_start, t), :]}

        chains = [(h, b) for b in ("d", "p") for h in range(2)]
        zs = {c: lax.dot_general(q_heads[c[0]], keys[c[1]], nt_dims, preferred_element_type=F32)
              for c in chains}
        sps, cs, atts, outs = {}, {}, {}, {}

        def keep(c, x):
            return jnp.where(strict, x, 0.0) if c[1] == "d" else jnp.where(has_prev, x, 0.0)

        def sp_of(c):
            return keep(c, _softplus2(zs[c])).astype(BF16)

        def att_of(c):
            e = zs[c] - cs[c]
            if c[1] == "p":
                e = e - cs[(c[0], "d")][:, 0:1]
            return keep(c, jnp.exp2(e)).astype(BF16)

        sps[chains[0]] = sp_of(chains[0])
        for n, c in enumerate(chains):
            cs[c] = jnp.dot(sps[c], tri, preferred_element_type=F32)
            if n + 1 < len(chains):
                sps[chains[n + 1]] = sp_of(chains[n + 1])
        r0, r1 = [cs[(h, "d")][:, 0:1] + cs[(h, "p")][:, 0:1] for h in range(2)]
        r_min = jnp.min(jnp.minimum(r0, r1))
        atts[chains[0]] = att_of(chains[0])
        for n, c in enumerate(chains):
            outs[c] = jnp.dot(atts[c], vals[c[1]], preferred_element_type=F32)
            if n + 1 < len(chains):
                atts[chains[n + 1]] = att_of(chains[n + 1])
        acc = jnp.where(first_head, outs[(0, "d")] + outs[(0, "p")], outs[(1, "d")] + outs[(1, "p")])

        def block(kb, acc, r0, r1):
            start = pl.multiple_of(kb * t, t)
            k = k_ref[0, pl.ds(start, t), :]
            v = v_ref[0, pl.ds(start, t), :]
            z = [lax.dot_general(qh, k, nt_dims, preferred_element_type=F32) for qh in q_heads]
            sp0 = _softplus2(z[0]).astype(BF16)
            c0 = jnp.dot(sp0, tri, preferred_element_type=F32)
            sp1 = _softplus2(z[1]).astype(BF16)
            c1 = jnp.dot(sp1, tri, preferred_element_type=F32)
            att0 = jnp.exp2(z[0] - c0 - r0).astype(BF16)
            o0 = jnp.dot(att0, v, preferred_element_type=F32)
            att1 = jnp.exp2(z[1] - c1 - r1).astype(BF16)
            o1 = jnp.dot(att1, v, preferred_element_type=F32)
            r0n = r0 + c0[:, 0:1]
            r1n = r1 + c1[:, 0:1]
            return (acc + jnp.where(first_head, o0, o1), r0n, r1n,
                    jnp.min(jnp.minimum(r0n, r1n)))

        def cond(c):
            return jnp.logical_and(c[0] >= 0, c[4] < ATT_DEAD)

        def body(c):
            kb, acc, r0, r1, _ = c
            acc, r0, r1, r_min = block(kb, acc, r0, r1)
            return kb - 1, acc, r0, r1, r_min

        carry = lax.while_loop(cond, body, (i - 2, acc, r0, r1, r_min))
        o_ref[0, pl.ds(q_start, t), :] = carry[1].astype(o_ref.dtype)
        return 0

    lax.fori_loop(0, nq, tile, 0)


def _attention(qkv, bsz, seq):
    d = qkv.shape[-1] // 3
    qkv = qkv.reshape(bsz, seq, 3 * d)
    t = ATT_T
    npair = d // LANES
    return pl.pallas_call(
        _attention_kernel,
        grid=(bsz, npair),
        in_specs=[pl.BlockSpec((1, seq, LANES), lambda b, p: (b, 0, p)),
                  pl.BlockSpec((1, seq, LANES), lambda b, p: (b, 0, npair + p)),
                  pl.BlockSpec((1, seq, LANES), lambda b, p: (b, 0, 2 * npair + p))],
        out_specs=pl.BlockSpec((1, seq, LANES), lambda b, p: (b, 0, p)),
        out_shape=jax.ShapeDtypeStruct((bsz, seq, d), BF16),
        compiler_params=pltpu.CompilerParams(
            dimension_semantics=("arbitrary", "arbitrary"), vmem_limit_bytes=VMEM_LIMIT),
        name="sb_attention",
    )(qkv, qkv, qkv)


def _proj_ln_kernel(y_ref, x_ref, w_ref, ln_g_ref, ln_b_ref, o_ref):
    mix = jnp.dot(y_ref[...], w_ref[...], preferred_element_type=F32)
    _store_row_tiles(o_ref, _layer_norm(DN_ALPHA * x_ref[...] + mix, ln_g_ref[...], ln_b_ref[...]))


def _proj_ln(y, xt, w, ln_g, ln_b):
    t, d = xt.shape
    tm = PROJ_TM
    return pl.pallas_call(
        _proj_ln_kernel,
        grid=(t // tm,),
        in_specs=[pl.BlockSpec((tm, d), lambda i: (i, 0)),
                  pl.BlockSpec((tm, d), lambda i: (i, 0)),
                  pl.BlockSpec((d, d), lambda i: (0, 0)),
                  pl.BlockSpec((1, d), lambda i: (0, 0)),
                  pl.BlockSpec((1, d), lambda i: (0, 0))],
        out_specs=pl.BlockSpec((tm * SUBLANES, LANES), lambda i: (i, 0)),
        out_shape=jax.ShapeDtypeStruct((t * SUBLANES, LANES), F32),
        compiler_params=pltpu.CompilerParams(
            dimension_semantics=("arbitrary",), vmem_limit_bytes=VMEM_LIMIT),
        name="out_proj_ln",
    )(y, xt, w.astype(BF16), ln_g.reshape(1, -1), ln_b.reshape(1, -1))


def _sb_mixer(x, w_qkv, w_out, ln_g, ln_b):
    bsz, seq, d = x.shape
    xt = x.reshape(-1, d)
    o = _attention(_qkv(xt, w_qkv), bsz, seq)
    return _proj_ln(o.reshape(-1, d), xt, w_out, ln_g, ln_b)


def kernel(x, even_w_in, even_conv_w, even_sgu_ln_g, even_sgu_ln_b, even_sgu_w_s, even_sgu_b_s, even_w_out, odd_w_qkv, odd_w_out, mix_ln_g, mix_ln_b, moe_w_group, moe_b_group, moe_w_router, moe_b_router, moe_w1, moe_w3, moe_w2, ffn_ln_g, ffn_ln_b):
    shape = x.shape
    for layer in range(DEPTH):
        i = layer // 2
        if layer % 2 == 0:
            x3 = _mixer0(x, even_w_in[i], even_conv_w[i], even_sgu_ln_g[i], even_sgu_ln_b[i],
                         even_sgu_w_s[i], even_sgu_b_s[i], even_w_out[i], mix_ln_g[layer], mix_ln_b[layer])
        else:
            x3 = _sb_mixer(x, odd_w_qkv[i], odd_w_out[i], mix_ln_g[layer], mix_ln_b[layer])
        x = _moe_layer(x3, moe_w_group[layer], moe_b_group[layer], moe_w_router[layer],
                       moe_b_router[layer], moe_w1, moe_w3, moe_w2, layer,
                       ffn_ln_g[layer], ffn_ln_b[layer]).reshape(shape)
    return x
```

```python
import functools

import jax
import jax.numpy as jnp
from jax import lax
from jax.experimental import pallas as pl
from jax.experimental.pallas import tpu as pltpu

F32 = jnp.float32
BF16 = jnp.bfloat16

DEPTH = 2
DN_ALPHA = (2 * DEPTH) ** 0.25
LN_EPS = 1e-5

CONV_WIDTH = 512
CONV_K = 3
SGU_WIDTH = 512
SGU_HEADS = 8
SGU_HEAD_DIM = 64
SGU_BLOCK = 128
CHUNK = 64
SB_HEADS = 16
SB_HEAD_DIM = 64
N_GROUPS = 4
EXPERTS_PER_GROUP = 4
N_EXPERTS = N_GROUPS * EXPERTS_PER_GROUP

LANES = 128
SUBLANES = 8
VMEM_LIMIT = 56 * 1024 * 1024

MIX_TM = 512
PROJ_TM = 1024
ROUTE_TM = 1024
ROUTE_ROWS = 8
PLAN_COLS = 16
MOE_TR = 256
GATHER_SLOTS = 3
UNSORT_TM = 512
ATT_T = 256
LOG2_E = 1.4426950408889634
ATT_DEAD = 150.0


def _layer_norm(r, g, b):
    mu = jnp.mean(r, axis=-1, keepdims=True)
    c = r - mu
    var = jnp.mean(c * c, axis=-1, keepdims=True)
    return c * lax.rsqrt(var + LN_EPS) * g + b


def _gelu_tanh(x):
    return 0.5 * x * (1.0 + jnp.tanh(0.7978845608028654 * (x + 0.044715 * (x * x * x))))


def _store_row_tiles(ref, val):
    n, d = val.shape
    assert d == SUBLANES * LANES
    for c in range(SUBLANES):
        ref[pl.ds(c, n, stride=SUBLANES), :] = val[:, c * LANES:(c + 1) * LANES]


def _load_row_tiles(ref, n):
    return jnp.concatenate([ref[pl.ds(c, n, stride=SUBLANES), :] for c in range(SUBLANES)], axis=1)


def _mixer0_kernel(x_ref, w_in_ref, conv_w_ref, sgu_g_ref, sgu_b_ref, ws_ref, bs_ref, w_out_ref,
                   ln_g_ref, ln_b_ref, rw_ref, rb_ref, o_ref, route_ref, tail_ref, y_ref):
    tm = x_ref.shape[1]

    @pl.when(pl.program_id(1) == 0)
    def _():
        tail_ref[...] = jnp.zeros_like(tail_ref)

    x = x_ref[0]
    xb = x.astype(BF16)
    cw = CONV_WIDTH
    bch = jnp.dot(xb, w_in_ref[:, :3 * cw], preferred_element_type=F32)
    b_gate, c_gate, h = bch[:, :cw], bch[:, cw:2 * cw], bch[:, 2 * cw:]
    u = c_gate * h
    cat = jnp.concatenate([tail_ref[...], u], axis=0)
    tail_ref[...] = u[tm - 8:, :]
    u1 = cat[7:7 + tm, :]
    u2 = cat[6:6 + tm, :]
    conv = conv_w_ref[0:1, :] * u2 + conv_w_ref[1:2, :] * u1 + conv_w_ref[2:3, :] * u
    y_ref[:, :cw] = (b_gate * conv).astype(BF16)

    z = _gelu_tanh(jnp.dot(xb, w_in_ref[:, 3 * cw:], preferred_element_type=F32))
    z_u, z_v = z[:, :SGU_WIDTH], z[:, SGU_WIDTH:]
    v = _layer_norm(z_v, sgu_g_ref[...], sgu_b_ref[...])
    lane = lax.broadcasted_iota(jnp.int32, (tm, LANES), 1)
    first_head = lane < SGU_HEAD_DIM
    nblk = tm // SGU_BLOCK
    for j in range(SGU_HEADS // 2):
        vp = v[:, j * LANES:(j + 1) * LANES]
        v_a = jnp.where(first_head, vp, 0.0).astype(BF16)
        v_b = jnp.where(first_head, 0.0, vp).astype(BF16)
        rhs = jnp.concatenate(
            [jnp.concatenate([v_a[n * SGU_BLOCK:(n + 1) * SGU_BLOCK],
                              v_b[n * SGU_BLOCK:(n + 1) * SGU_BLOCK]], axis=0)
             for n in range(nblk)], axis=1)
        mixed = jnp.dot(ws_ref[j], rhs, preferred_element_type=F32)
        bias = bs_ref[:, j * LANES:(j + 1) * LANES]
        for n in range(nblk):
            rows = slice(n * SGU_BLOCK, (n + 1) * SGU_BLOCK)
            gate = z_u[rows, j * LANES:(j + 1) * LANES]
            y_ref[rows, cw + j * LANES:cw + (j + 1) * LANES] = (
                gate * (mixed[:, n * LANES:(n + 1) * LANES] + bias)).astype(BF16)

    mix = jnp.dot(y_ref[...], w_out_ref[...], preferred_element_type=F32)
    out = _layer_norm(DN_ALPHA * x + mix, ln_g_ref[...], ln_b_ref[...])
    _store_row_tiles(o_ref, out)
    _route(out, rw_ref, rb_ref, route_ref)


def _mixer0(x, w_in, conv_w, sgu_g, sgu_b, w_s, b_s, w_out, ln_g, ln_b, route_w, route_b):
    bsz, seq, d = x.shape
    tm = MIX_TM
    pos = jnp.arange(SGU_BLOCK)
    chunk_causal = (pos[None, :] // CHUNK) <= (pos[:, None] // CHUNK)
    w_masked = jnp.where(chunk_causal[None], w_s, 0.0).astype(BF16)
    ws_pairs = jnp.concatenate([w_masked[0::2], w_masked[1::2]], axis=2)
    bias = jnp.repeat(jnp.transpose(b_s), SGU_HEAD_DIM, axis=1)
    full = lambda shape: pl.BlockSpec(shape, lambda b, s: (0,) * len(shape))
    return pl.pallas_call(
        _mixer0_kernel,
        grid=(bsz, seq // tm),
        in_specs=[
            pl.BlockSpec((1, tm, d), lambda b, s: (b, s, 0)),
            full(w_in.shape), full(conv_w.shape), full((1, SGU_WIDTH)), full((1, SGU_WIDTH)),
            full(ws_pairs.shape), full(bias.shape), full(w_out.shape),
            full((1, d)), full((1, d)), full(route_w.shape), full(route_b.shape),
        ],
        out_specs=[pl.BlockSpec((tm * SUBLANES, LANES), lambda b, s: (b * (seq // tm) + s, 0)),
                   pl.BlockSpec((ROUTE_ROWS, tm), lambda b, s: (0, b * (seq // tm) + s))],
        out_shape=[jax.ShapeDtypeStruct((bsz * seq * SUBLANES, LANES), F32),
                   jax.ShapeDtypeStruct((ROUTE_ROWS, bsz * seq), F32)],
        scratch_shapes=[pltpu.VMEM((8, CONV_WIDTH), F32), pltpu.VMEM((tm, d), BF16)],
        compiler_params=pltpu.CompilerParams(
            dimension_semantics=("arbitrary", "arbitrary"), vmem_limit_bytes=VMEM_LIMIT),
        name="mixer0",
    )(x, w_in.astype(BF16), conv_w, sgu_g.reshape(1, -1), sgu_b.reshape(1, -1), ws_pairs, bias,
      w_out.astype(BF16), ln_g.reshape(1, -1), ln_b.reshape(1, -1), route_w, route_b)


def _route(x, w_ref, b_ref, route_ref):
    xh = x.astype(BF16)
    xl = (x - xh.astype(F32)).astype(BF16)
    w = w_ref[...]
    wh = w.astype(BF16)
    wl = (w - wh.astype(F32)).astype(BF16)
    logits_t = (jnp.dot(xh, wh, preferred_element_type=F32)
                + jnp.dot(xl, wh, preferred_element_type=F32)
                + jnp.dot(xh, wl, preferred_element_type=F32))
    logits = jnp.transpose(logits_t)[:b_ref.shape[0], :] + b_ref[...]
    gl = [logits[g:g + 1, :] for g in range(N_GROUPS)]
    g_max = functools.reduce(jnp.maximum, gl)
    g_top = 1.0 / functools.reduce(lambda a, b: a + b, [jnp.exp(l - g_max) for l in gl])
    g_idx = jnp.full_like(g_max, N_GROUPS).astype(jnp.int32)
    for g in reversed(range(N_GROUPS)):
        g_idx = jnp.where(gl[g] == g_max, g, g_idx)
    el = []
    for e in range(EXPERTS_PER_GROUP):
        acc = jnp.zeros_like(g_max)
        for g in range(N_GROUPS):
            r = N_GROUPS + g * EXPERTS_PER_GROUP + e
            acc = jnp.where(g_idx == g, logits[r:r + 1, :], acc)
        el.append(acc)
    m1 = functools.reduce(jnp.maximum, el)
    i1 = jnp.full_like(g_idx, EXPERTS_PER_GROUP)
    for e in reversed(range(EXPERTS_PER_GROUP)):
        i1 = jnp.where(el[e] == m1, e, i1)
    rest = [jnp.where(i1 == e, -jnp.inf, el[e]) for e in range(EXPERTS_PER_GROUP)]
    m2 = functools.reduce(jnp.maximum, rest)
    i2 = jnp.full_like(g_idx, EXPERTS_PER_GROUP)
    for e in reversed(range(EXPERTS_PER_GROUP)):
        i2 = jnp.where(rest[e] == m2, e, i2)
    p2 = jnp.exp(m2 - m1)
    w1 = g_top / (1.0 + p2)
    w2 = g_top * p2 / (1.0 + p2)
    for g in range(N_GROUPS):
        route_ref[g:g + 1, :] = jnp.where(g_idx == g, 1.0, 0.0)
    for e in range(EXPERTS_PER_GROUP):
        route_ref[N_GROUPS + e:N_GROUPS + e + 1, :] = jnp.where(
            i1 == e, w1, jnp.where(i2 == e, w2, 0.0))


ROUTE_W_ROWS = 32


def _route_params(w_group, b_group, w_router, b_router):
    d = w_group.shape[0]
    w_all = jnp.concatenate(
        [jnp.transpose(w_group), jnp.transpose(w_router, (0, 2, 1)).reshape(N_EXPERTS, d)], axis=0)
    w_t = jnp.transpose(jnp.pad(w_all, ((0, LANES - w_all.shape[0]), (0, 0))))
    b_all = jnp.pad(jnp.concatenate([b_group, b_router.reshape(-1)]),
                    (0, ROUTE_W_ROWS - N_GROUPS - N_EXPERTS))
    return w_t, b_all.reshape(ROUTE_W_ROWS, 1)


def _plan_kernel(route_ref, ut_ref, plan_ref, src_ref, dst_ref, tg_ref, cum_s, m_s, vals_s, cin_s):
    t = route_ref.shape[1]
    rt = ut_ref.shape[0]
    nk = t // rt
    nt, tr, _ = plan_ref.shape
    lane = lax.broadcasted_iota(jnp.int32, (1, rt), 1)

    def cum_body(k, carry):
        off = pl.multiple_of(k * rt, rt)
        route = route_ref[:, pl.ds(off, rt)]
        cum = jnp.dot(route.astype(BF16), ut_ref[...], preferred_element_type=F32) + carry
        comb = route[N_GROUPS:, :]
        b0 = comb.astype(BF16).astype(F32)
        b1 = (comb - b0).astype(BF16).astype(F32)
        b2 = (comb - b0 - b1).astype(BF16).astype(F32)
        tok = off + lane
        pieces = ([b0[e:e + 1] for e in range(EXPERTS_PER_GROUP)]
                  + [b1[e:e + 1] for e in range(EXPERTS_PER_GROUP)]
                  + [b2[e:e + 1] for e in range(EXPERTS_PER_GROUP)]
                  + [(tok >> 8).astype(F32), (tok & 255).astype(F32)])
        for r, piece in enumerate(pieces):
            vals_s[r:r + 1, pl.ds(off, rt)] = piece
        for r in range(len(pieces), PLAN_COLS):
            vals_s[r:r + 1, pl.ds(off, rt)] = jnp.zeros((1, rt), F32)
        for g in range(N_GROUPS):
            cum_s[g, :, pl.ds(off, rt)] = cum[g:g + 1, :]
            m_s[g, :, pl.ds(off, rt)] = route[g:g + 1, :]
            cin_s[k * N_GROUPS + g] = carry[g, 0].astype(jnp.int32)
        return cum[:, rt - 1:rt]

    total = lax.fori_loop(0, nk, cum_body, jnp.zeros((ROUTE_ROWS, 1), F32))
    tile_base = [jnp.int32(0)]
    for g in range(N_GROUPS):
        cnt = total[g, 0].astype(jnp.int32)
        cin_s[nk * N_GROUPS + g] = cnt
        tile_base.append(tile_base[-1] + (cnt + tr - 1) // tr)

    def dst_body(k, _):
        off = pl.multiple_of(k * rt, rt)
        pos = jnp.zeros((1, rt), F32)
        for g in range(N_GROUPS):
            pos = pos + m_s[g, :, pl.ds(off, rt)] * (
                cum_s[g, :, pl.ds(off, rt)] + (tile_base[g] * tr - 1).astype(F32))
        dst_ref[:, pl.ds(off, rt)] = (pos * SUBLANES).astype(jnp.int32)
        return 0

    lax.fori_loop(0, nk, dst_body, 0)

    row_iota = lax.broadcasted_iota(jnp.int32, (tr, rt), 0).astype(F32)

    def tile_body(i, k_prev):
        g = ((i >= tile_base[1]).astype(jnp.int32) + (i >= tile_base[2]).astype(jnp.int32)
             + (i >= tile_base[3]).astype(jnp.int32))
        base = jnp.where(g == 0, tile_base[0],
                         jnp.where(g == 1, tile_base[1], jnp.where(g == 2, tile_base[2], tile_base[3])))
        r0 = (i - base) * tr
        tg_ref[i] = g
        target = row_iota + (r0 + 1).astype(F32)

        k_first = lax.while_loop(
            lambda k: jnp.logical_and(k < nk, cin_s[jnp.minimum(k + 1, nk) * N_GROUPS + g] <= r0),
            lambda k: k + 1, jnp.where(i == base, 0, k_prev))

        def more(c):
            return jnp.logical_and(c[0] < nk, cin_s[c[0] * N_GROUPS + g] < r0 + tr)

        def pick(c):
            k, acc = c
            off = pl.multiple_of(k * rt, rt)
            onehot = jnp.where(cum_s[g, :, pl.ds(off, rt)] == target,
                               m_s[g, :, pl.ds(off, rt)], 0.0).astype(BF16)
            vals = vals_s[:, pl.ds(off, rt)].astype(BF16)
            return k + 1, acc + lax.dot_general(onehot, vals, (((1,), (1,)), ((), ())),
                                                preferred_element_type=F32)

        k_end, acc = lax.while_loop(more, pick, (k_first, jnp.zeros((tr, PLAN_COLS), F32)))
        plan_ref[i] = acc
        digits = jnp.transpose(jnp.concatenate(
            [acc, jnp.zeros((tr, LANES - PLAN_COLS), F32)], axis=1))
        n = 3 * EXPERTS_PER_GROUP
        src_ref[i] = ((digits[n:n + 1, :] * 256.0 + digits[n + 1:n + 2, :]) * SUBLANES).astype(jnp.int32)
        return jnp.maximum(k_end - 1, k_first)

    lax.fori_loop(0, nt, tile_body, jnp.int32(0))


def _plan(route):
    t = route.shape[1]
    rt = ROUTE_TM
    tr = MOE_TR
    nt = t // tr + N_GROUPS
    idx = jnp.arange(rt)
    ut = (idx[:, None] <= idx[None, :]).astype(BF16)
    return pl.pallas_call(
        _plan_kernel,
        in_specs=[pl.BlockSpec(memory_space=pltpu.VMEM), pl.BlockSpec(memory_space=pltpu.VMEM)],
        out_specs=[pl.BlockSpec(memory_space=pltpu.VMEM), pl.BlockSpec(memory_space=pltpu.VMEM),
                   pl.BlockSpec(memory_space=pltpu.VMEM), pl.BlockSpec(memory_space=pltpu.SMEM)],
        out_shape=[jax.ShapeDtypeStruct((nt, tr, PLAN_COLS), F32),
                   jax.ShapeDtypeStruct((nt, 1, tr), jnp.int32),
                   jax.ShapeDtypeStruct((1, t), jnp.int32),
                   jax.ShapeDtypeStruct((nt,), jnp.int32)],
        scratch_shapes=[pltpu.VMEM((N_GROUPS, 1, t), F32), pltpu.VMEM((N_GROUPS, 1, t), F32),
                        pltpu.VMEM((PLAN_COLS, t), F32),
                        pltpu.SMEM(((t // rt + 1) * N_GROUPS,), jnp.int32)],
        compiler_params=pltpu.CompilerParams(vmem_limit_bytes=VMEM_LIMIT),
        name="moe_plan",
    )(route, ut)


def _start_row_gather(idx_ref, first, src_ref, buf, slot, sem):
    if not isinstance(slot, int):
        for s in range(buf.shape[0]):
            @pl.when(slot == s)
            def _():
                _start_row_gather(idx_ref, first, src_ref, buf, s, sem)
        return
    for r in range(buf.shape[1] // SUBLANES):
        row = pl.multiple_of(idx_ref[first + r], SUBLANES)
        pltpu.make_async_copy(src_ref.at[pl.ds(row, SUBLANES), :],
                              buf.at[slot, pl.ds(r * SUBLANES, SUBLANES), :], sem.at[slot]).start()


def _wait_row_gather(src_ref, buf, slot, sem):
    pltpu.make_async_copy(src_ref.at[pl.ds(0, buf.shape[1]), :], buf.at[slot], sem.at[slot]).wait()


def _ffn_kernel(layer, tg_ref, src_ref, x_ref, plan_ref, w1_ref, w3_ref, w2_ref, ln_g_ref, ln_b_ref,
                o_ref, xbuf, sem, wb1, wb3, wb2, stage_in, stage_out, wsem):
    i = pl.program_id(0)
    last = pl.num_programs(0) - 1
    nslot = xbuf.shape[0]
    tr = xbuf.shape[1] // SUBLANES
    slot = i % nslot
    group = tg_ref[i]

    @pl.when(jnp.logical_or(i == 0, group != tg_ref[jnp.maximum(i - 1, 0)]))
    def _():
        chunks = [(w_hbm, wb, stage, e)
                  for w_hbm, wb, stage in ((w1_ref, wb1, stage_in), (w3_ref, wb3, stage_in),
                                           (w2_ref, wb2, stage_out))
                  for e in range(EXPERTS_PER_GROUP)]

        def copy(j):
            w_hbm, _, stage, e = chunks[j]
            return pltpu.make_async_copy(w_hbm.at[layer, group, e], stage.at[j % 2], wsem.at[j % 2])

        copy(0).start()
        for j, (_, wb, stage, e) in enumerate(chunks):
            if j + 1 < len(chunks):
                copy(j + 1).start()
            copy(j).wait()
            wb[e] = stage[j % 2].astype(BF16)

    @pl.when(i == 0)
    def _():
        for ahead in range(nslot - 1):
            _start_row_gather(src_ref, jnp.minimum(ahead, last) * tr, x_ref, xbuf, ahead, sem)

    _wait_row_gather(x_ref, xbuf, slot, sem)
    x = _load_row_tiles(xbuf.at[slot], tr)
    xb = x.astype(BF16)
    plan = plan_ref[0]
    y = jnp.zeros(x.shape, F32)
    for e in range(EXPERTS_PER_GROUP):
        n = EXPERTS_PER_GROUP
        c = plan[:, e:e + 1] + plan[:, n + e:n + e + 1] + plan[:, 2 * n + e:2 * n + e + 1]
        h1 = jnp.dot(xb, wb1[e], preferred_element_type=F32)
        h3 = jnp.dot(xb, wb3[e], preferred_element_type=F32)
        hid = (h1 * jax.nn.sigmoid(h1)) * h3 * c
        y = y + jnp.dot(hid.astype(BF16), wb2[e], preferred_element_type=F32)
    _store_row_tiles(o_ref, _layer_norm(DN_ALPHA * x + y, ln_g_ref[...], ln_b_ref[...]))

    _start_row_gather(src_ref, jnp.minimum(i + nslot - 1, last) * tr, x_ref, xbuf,
                      (i + nslot - 1) % nslot, sem)

    @pl.when(i == last)
    def _():
        for ahead in range(1, nslot):
            _wait_row_gather(x_ref, xbuf, (i + ahead) % nslot, sem)


def _ffn(tile_group, src, x3, plan, w1, w3, w2, layer, ln_g, ln_b):
    nt, tr, _ = plan.shape
    d, f = w1.shape[-2:]
    n = EXPERTS_PER_GROUP
    hbm = pl.BlockSpec(memory_space=pl.ANY)
    return pl.pallas_call(
        functools.partial(_ffn_kernel, layer),
        grid_spec=pltpu.PrefetchScalarGridSpec(
            num_scalar_prefetch=2,
            grid=(nt,),
            in_specs=[hbm,
                      pl.BlockSpec((1, tr, PLAN_COLS), lambda i, tg, src: (i, 0, 0)),
                      hbm, hbm, hbm,
                      pl.BlockSpec((1, d), lambda i, tg, src: (0, 0)),
                      pl.BlockSpec((1, d), lambda i, tg, src: (0, 0))],
            out_specs=pl.BlockSpec((tr * SUBLANES, LANES), lambda i, tg, src: (i, 0)),
            scratch_shapes=[pltpu.VMEM((GATHER_SLOTS, tr * SUBLANES, LANES), F32),
                            pltpu.SemaphoreType.DMA((GATHER_SLOTS,)),
                            pltpu.VMEM((n, d, f), BF16), pltpu.VMEM((n, d, f), BF16),
                            pltpu.VMEM((n, f, d), BF16),
                            pltpu.VMEM((2, d, f), F32), pltpu.VMEM((2, f, d), F32),
                            pltpu.SemaphoreType.DMA((2,))]),
        out_shape=jax.ShapeDtypeStruct((nt * tr * SUBLANES, LANES), F32),
        compiler_params=pltpu.CompilerParams(
            dimension_semantics=("arbitrary",), vmem_limit_bytes=VMEM_LIMIT),
        name="moe_ffn",
    )(tile_group, src, x3, plan, w1, w3, w2, ln_g.reshape(1, -1), ln_b.reshape(1, -1))


def _unsort_kernel(dst_ref, y_ref, o_ref, buf, sem):
    i = pl.program_id(0)
    last = pl.num_programs(0) - 1
    nslot = buf.shape[0]
    tm = buf.shape[1] // SUBLANES
    slot = i % nslot

    @pl.when(i == 0)
    def _():
        for ahead in range(nslot - 1):
            _start_row_gather(dst_ref, jnp.minimum(ahead, last) * tm, y_ref, buf, ahead, sem)

    _wait_row_gather(y_ref, buf, slot, sem)
    o_ref[...] = _load_row_tiles(buf.at[slot], tm)
    _start_row_gather(dst_ref, jnp.minimum(i + nslot - 1, last) * tm, y_ref, buf,
                      (i + nslot - 1) % nslot, sem)

    @pl.when(i == last)
    def _():
        for ahead in range(1, nslot):
            _wait_row_gather(y_ref, buf, (i + ahead) % nslot, sem)


def _unsort(dst, y3, t):
    d = SUBLANES * LANES
    tm = UNSORT_TM
    return pl.pallas_call(
        _unsort_kernel,
        grid_spec=pltpu.PrefetchScalarGridSpec(
            num_scalar_prefetch=1,
            grid=(t // tm,),
            in_specs=[pl.BlockSpec(memory_space=pl.ANY)],
            out_specs=pl.BlockSpec((tm, d), lambda i, dst: (i, 0)),
            scratch_shapes=[pltpu.VMEM((GATHER_SLOTS, tm * SUBLANES, LANES), F32),
                            pltpu.SemaphoreType.DMA((GATHER_SLOTS,))]),
        out_shape=jax.ShapeDtypeStruct((t, d), F32),
        compiler_params=pltpu.CompilerParams(
            dimension_semantics=("arbitrary",), vmem_limit_bytes=VMEM_LIMIT),
        name="moe_unsort",
    )(dst, y3)


def _moe_layer(x3, route, w1, w3, w2, layer, ln_g, ln_b):
    plan, src, dst, tile_group = _plan(route)
    y3 = _ffn(tile_group, src.reshape(-1), x3, plan, w1, w3, w2, layer, ln_g, ln_b)
    return _unsort(dst.reshape(-1), y3, x3.shape[0] // SUBLANES)


def _qkv_kernel(x_ref, w_ref, o_ref):
    d = x_ref.shape[1]
    xb = x_ref[...].astype(BF16)
    for part in range(3):
        y = jnp.dot(xb, w_ref[:, part * d:(part + 1) * d], preferred_element_type=F32)
        if part == 0:
            y = y * (SB_HEAD_DIM ** -0.5 * LOG2_E)
        o_ref[:, part * d:(part + 1) * d] = y.astype(BF16)


def _qkv(xt, w_qkv):
    t, d = xt.shape
    tm = PROJ_TM
    return pl.pallas_call(
        _qkv_kernel,
        grid=(t // tm,),
        in_specs=[pl.BlockSpec((tm, d), lambda i: (i, 0)),
                  pl.BlockSpec((d, 3 * d), lambda i: (0, 0))],
        out_specs=pl.BlockSpec((tm, 3 * d), lambda i: (i, 0)),
        out_shape=jax.ShapeDtypeStruct((t, 3 * d), BF16),
        compiler_params=pltpu.CompilerParams(
            dimension_semantics=("arbitrary",), vmem_limit_bytes=VMEM_LIMIT),
        name="qkv_proj",
    )(xt, w_qkv.astype(BF16))


def _softplus2(z2):
    return jnp.maximum(z2, 0.0) + jnp.log2(1.0 + jnp.exp2(-jnp.abs(z2)))


def _attention_kernel(q_ref, k_ref, v_ref, o_ref):
    t = ATT_T
    nq = q_ref.shape[1] // t
    nt_dims = (((1,), (1,)), ((), ()))
    lane = lax.broadcasted_iota(jnp.int32, (t, LANES), 1)
    first_head = lane < SB_HEAD_DIM
    row = lax.broadcasted_iota(jnp.int32, (t, t), 0)
    col = lax.broadcasted_iota(jnp.int32, (t, t), 1)
    tri = jnp.where(row >= col, 1.0, 0.0).astype(BF16)
    strict = col < row

    def tile(i, _):
        q_start = pl.multiple_of(i * t, t)
        p_start = pl.multiple_of(jnp.maximum(i - 1, 0) * t, t)
        has_prev = i > 0
        q = q_ref[0, pl.ds(q_start, t), :]
        q_heads = (jnp.where(first_head, q, jnp.zeros_like(q)),
                   jnp.where(first_head, jnp.zeros_like(q), q))
        keys = {"d": k_ref[0, pl.ds(q_start, t), :], "p": k_ref[0, pl.ds(p_start, t), :]}
        vals = {"d": v_ref[0, pl.ds(q_start, t), :], "p": v_ref[0, pl.ds(p_start, t), :]}

        chains = [(h, b) for b in ("d", "p") for h in range(2)]
        zs = {c: lax.dot_general(q_heads[c[0]], keys[c[1]], nt_dims, preferred_element_type=F32)
              for c in chains}
        sps, cs, atts, outs = {}, {}, {}, {}

        def keep(c, x):
            return jnp.where(strict, x, 0.0) if c[1] == "d" else jnp.where(has_prev, x, 0.0)

        def sp_of(c):
            return keep(c, _softplus2(zs[c])).astype(BF16)

        def att_of(c):
            e = zs[c] - cs[c]
            if c[1] == "p":
                e = e - cs[(c[0], "d")][:, 0:1]
            return keep(c, jnp.exp2(e)).astype(BF16)

        sps[chains[0]] = sp_of(chains[0])
        for n, c in enumerate(chains):
            cs[c] = jnp.dot(sps[c], tri, preferred_element_type=F32)
            if n + 1 < len(chains):
                sps[chains[n + 1]] = sp_of(chains[n + 1])
        r0, r1 = [cs[(h, "d")][:, 0:1] + cs[(h, "p")][:, 0:1] for h in range(2)]
        r_min = jnp.min(jnp.minimum(r0, r1))
        atts[chains[0]] = att_of(chains[0])
        for n, c in enumerate(chains):
            outs[c] = jnp.dot(atts[c], vals[c[1]], preferred_element_type=F32)
            if n + 1 < len(chains):
                atts[chains[n + 1]] = att_of(chains[n + 1])
        acc = jnp.where(first_head, outs[(0, "d")] + outs[(0, "p")], outs[(1, "d")] + outs[(1, "p")])

        def block(kb, acc, r0, r1):
            start = pl.multiple_of(kb * t, t)
            k = k_ref[0, pl.ds(start, t), :]
            v = v_ref[0, pl.ds(start, t), :]
            z = [lax.dot_general(qh, k, nt_dims, preferred_element_type=F32) for qh in q_heads]
            sp0 = _softplus2(z[0]).astype(BF16)
            c0 = jnp.dot(sp0, tri, preferred_element_type=F32)
            sp1 = _softplus2(z[1]).astype(BF16)
            c1 = jnp.dot(sp1, tri, preferred_element_type=F32)
            att0 = jnp.exp2(z[0] - c0 - r0).astype(BF16)
            o0 = jnp.dot(att0, v, preferred_element_type=F32)
            att1 = jnp.exp2(z[1] - c1 - r1).astype(BF16)
            o1 = jnp.dot(att1, v, preferred_element_type=F32)
            r0n = r0 + c0[:, 0:1]
            r1n = r1 + c1[:, 0:1]
            return (acc + jnp.where(first_head, o0, o1), r0n, r1n,
                    jnp.min(jnp.minimum(r0n, r1n)))

        def cond(c):
            return jnp.logical_and(c[0] >= 0, c[4] < ATT_DEAD)

        def body(c):
            kb, acc, r0, r1, _ = c
            acc, r0, r1, r_min = block(kb, acc, r0, r1)
            return kb - 1, acc, r0, r1, r_min

        carry = lax.while_loop(cond, body, (i - 2, acc, r0, r1, r_min))
        o_ref[0, pl.ds(q_start, t), :] = carry[1].astype(o_ref.dtype)
        return 0

    lax.fori_loop(0, nq, tile, 0)


def _attention(qkv, bsz, seq):
    d = qkv.shape[-1] // 3
    qkv = qkv.reshape(bsz, seq, 3 * d)
    t = ATT_T
    npair = d // LANES
    return pl.pallas_call(
        _attention_kernel,
        grid=(bsz, npair),
        in_specs=[pl.BlockSpec((1, seq, LANES), lambda b, p: (b, 0, p)),
                  pl.BlockSpec((1, seq, LANES), lambda b, p: (b, 0, npair + p)),
                  pl.BlockSpec((1, seq, LANES), lambda b, p: (b, 0, 2 * npair + p))],
        out_specs=pl.BlockSpec((1, seq, LANES), lambda b, p: (b, 0, p)),
        out_shape=jax.ShapeDtypeStruct((bsz, seq, d), BF16),
        compiler_params=pltpu.CompilerParams(
            dimension_semantics=("arbitrary", "arbitrary"), vmem_limit_bytes=VMEM_LIMIT),
        name="sb_attention",
    )(qkv, qkv, qkv)


def _proj_ln_kernel(y_ref, x_ref, w_ref, ln_g_ref, ln_b_ref, rw_ref, rb_ref, o_ref, route_ref):
    mix = jnp.dot(y_ref[...], w_ref[...], preferred_element_type=F32)
    out = _layer_norm(DN_ALPHA * x_ref[...] + mix, ln_g_ref[...], ln_b_ref[...])
    _store_row_tiles(o_ref, out)
    _route(out, rw_ref, rb_ref, route_ref)


def _proj_ln(y, xt, w, ln_g, ln_b, route_w, route_b):
    t, d = xt.shape
    tm = PROJ_TM
    return pl.pallas_call(
        _proj_ln_kernel,
        grid=(t // tm,),
        in_specs=[pl.BlockSpec((tm, d), lambda i: (i, 0)),
                  pl.BlockSpec((tm, d), lambda i: (i, 0)),
                  pl.BlockSpec((d, d), lambda i: (0, 0)),
                  pl.BlockSpec((1, d), lambda i: (0, 0)),
                  pl.BlockSpec((1, d), lambda i: (0, 0)),
                  pl.BlockSpec(route_w.shape, lambda i: (0, 0)),
                  pl.BlockSpec(route_b.shape, lambda i: (0, 0))],
        out_specs=[pl.BlockSpec((tm * SUBLANES, LANES), lambda i: (i, 0)),
                   pl.BlockSpec((ROUTE_ROWS, tm), lambda i: (0, i))],
        out_shape=[jax.ShapeDtypeStruct((t * SUBLANES, LANES), F32),
                   jax.ShapeDtypeStruct((ROUTE_ROWS, t), F32)],
        compiler_params=pltpu.CompilerParams(
            dimension_semantics=("arbitrary",), vmem_limit_bytes=VMEM_LIMIT),
        name="out_proj_ln",
    )(y, xt, w.astype(BF16), ln_g.reshape(1, -1), ln_b.reshape(1, -1), route_w, route_b)


def _sb_mixer(x, w_qkv, w_out, ln_g, ln_b, route_w, route_b):
    bsz, seq, d = x.shape
    xt = x.reshape(-1, d)
    o = _attention(_qkv(xt, w_qkv), bsz, seq)
    return _proj_ln(o.reshape(-1, d), xt, w_out, ln_g, ln_b, route_w, route_b)


def kernel(x, even_w_in, even_conv_w, even_sgu_ln_g, even_sgu_ln_b, even_sgu_w_s, even_sgu_b_s, even_w_out, odd_w_qkv, odd_w_out, mix_ln_g, mix_ln_b, moe_w_group, moe_b_group, moe_w_router, moe_b_router, moe_w1, moe_w3, moe_w2, ffn_ln_g, ffn_ln_b):
    shape = x.shape
    for layer in range(DEPTH):
        i = layer // 2
        rw, rb = _route_params(moe_w_group[layer], moe_b_group[layer], moe_w_router[layer],
                               moe_b_router[layer])
        if layer % 2 == 0:
            x3, route = _mixer0(x, even_w_in[i], even_conv_w[i], even_sgu_ln_g[i], even_sgu_ln_b[i],
                                even_sgu_w_s[i], even_sgu_b_s[i], even_w_out[i], mix_ln_g[layer],
                                mix_ln_b[layer], rw, rb)
        else:
            x3, route = _sb_mixer(x, odd_w_qkv[i], odd_w_out[i], mix_ln_g[layer], mix_ln_b[layer], rw, rb)
        x = _moe_layer(x3, route, moe_w1, moe_w3, moe_w2, layer,
                       ffn_ln_g[layer], ffn_ln_b[layer]).reshape(shape)
    return x
```
